```python
import jax, jax.numpy as jnp
from jax import lax
import numpy as np

D_MODEL = 1024
BATCH = 2
SEQ = 8192
DEPTH = 4
DEC_BATCH = 128
DEC_SEQ = 8
PAST_LEN = 8192
PAGE_SIZE = 128

D_FF = 2816
D_CONV = 512
CONV_A_WIDTH = 3
D_LRU = 512
LRU_BLOCKS = 8
LRU_BLK = D_LRU // LRU_BLOCKS
CONV_B_WIDTH = 4
LRU_C = 8.0
MLA_HEADS = 8
Q_LORA = 384
KV_LORA = 256
NOPE_DIM = 64
ROPE_DIM = 32
V_DIM = 64
QK_DIM = NOPE_DIM + ROPE_DIM
MLA_WIDTH = MLA_HEADS * V_DIM
ROPE_THETA = 10000.0
Q_BLOCK = 128
N_MEM = 256
MEM_HEADS = 4
MEM_HD = 128
MEM_WIDTH = MEM_HEADS * MEM_HD
N_BRANCH = 4
BRANCH_WIDTHS = (D_CONV, D_LRU, MLA_WIDTH, MEM_WIDTH)
MIX_WIDTH = D_CONV + D_LRU + MLA_WIDTH + MEM_WIDTH
W_IN_SPLITS = (D_CONV, D_CONV, D_CONV, D_LRU, D_LRU, Q_LORA, KV_LORA, ROPE_DIM, MEM_WIDTH) + (D_MODEL,) * N_BRANCH
W_IN_COLS = sum(W_IN_SPLITS)
EPS = 1e-6
NEG = -1e30

kernel_name = "hybrid_conv_rglru_mla_memory_decoder_step"


def rmsnorm(x, g):
    xf = x.astype(jnp.float32)
    y = xf * lax.rsqrt(jnp.mean(xf * xf, axis=-1, keepdims=True) + EPS)
    return (y * g.astype(jnp.float32)).astype(x.dtype)


def swiglu(x, w_in, w_out):
    gate, up = jnp.split(x @ w_in, 2, axis=-1)
    return (jax.nn.silu(gate) * up) @ w_out


def rope(x, pos):
    half = ROPE_DIM // 2
    inv = ROPE_THETA ** (-jnp.arange(half, dtype=jnp.float32) / half)
    ang = pos.astype(jnp.float32)[:, None] * inv[None, :]
    cos = jnp.cos(ang)[None, :, None, :]
    sin = jnp.sin(ang)[None, :, None, :]
    x1 = x[..., :half].astype(jnp.float32)
    x2 = x[..., half:].astype(jnp.float32)
    return jnp.concatenate([x1 * cos - x2 * sin, x2 * cos + x1 * sin], axis=-1).astype(x.dtype)


def causal_dwconv(x, state, w):
    K = w.shape[0]
    T = x.shape[1]
    xp = jnp.concatenate([state.astype(x.dtype), x], axis=1)
    y = xp[:, 0:T] * w[0]
    for k in range(1, K):
        y = y + xp[:, k:k + T] * w[k]
    return y, xp[:, T:]


def _lin_combine(c1, c2):
    a1, b1 = c1
    a2, b2 = c2
    return a1 * a2, a2 * b1 + b2


def rg_lru(x, h0, gate_w, gate_b, lam):
    B, T, _ = x.shape
    xf = x.astype(jnp.float32)
    g = jnp.einsum('btnc,ncd->btnd', xf.reshape(B, T, LRU_BLOCKS, LRU_BLK), gate_w.astype(jnp.float32)) + gate_b.astype(jnp.float32)
    r = jax.nn.sigmoid(g[..., :LRU_BLK]).reshape(B, T, D_LRU)
    i = jax.nn.sigmoid(g[..., LRU_BLK:]).reshape(B, T, D_LRU)
    log_a = -LRU_C * r * jax.nn.softplus(-lam.astype(jnp.float32))
    a = jnp.exp(log_a)
    b = jnp.sqrt(-jnp.expm1(2.0 * log_a)) * (i * xf)
    b = b.at[:, 0].add(a[:, 0] * h0.astype(jnp.float32))
    _, h = lax.associative_scan(_lin_combine, (a, b), axis=1)
    return h.astype(x.dtype), h[:, -1].astype(x.dtype)


def attn_core(q, k, v, mask):
    scale = q.shape[-1] ** -0.5
    s = jnp.einsum('bqhd,bkhd->bhqk', q.astype(jnp.float32), k.astype(jnp.float32)) * scale
    if mask is not None:
        s = jnp.where(mask, s, NEG)
    p = jax.nn.softmax(s, axis=-1)
    return jnp.einsum('bhqk,bkhe->bqhe', p, v.astype(jnp.float32)).astype(q.dtype)


def mla_kv(lat, kr, w_ukv, k_gain):
    kv = jnp.einsum('btr,rhe->bthe', lat, w_ukv.reshape(KV_LORA, MLA_HEADS, NOPE_DIM + V_DIM))
    k_nope = kv[..., :NOPE_DIM]
    v = kv[..., NOPE_DIM:]
    k_pe = jnp.broadcast_to(kr[:, :, None, :].astype(k_nope.dtype), k_nope.shape[:-1] + (ROPE_DIM,))
    k = rmsnorm(jnp.concatenate([k_nope, k_pe], axis=-1), k_gain)
    return k, v


def mla_prompt_attend(q, lat, kr, w_ukv, k_gain):
    B, S = q.shape[0], q.shape[1]
    k, v = mla_kv(lat, kr, w_ukv, k_gain)
    nb = S // Q_BLOCK
    qb = q.reshape(B, nb, Q_BLOCK, MLA_HEADS, QK_DIM).swapaxes(0, 1)
    kpos = jnp.arange(S)

    def one(args):
        q_blk, bi = args
        qpos = bi * Q_BLOCK + jnp.arange(Q_BLOCK)
        return attn_core(q_blk, k, v, kpos[None, :] <= qpos[:, None])

    out = lax.map(one, (qb, jnp.arange(nb)))
    return out.swapaxes(0, 1).reshape(B, S, MLA_HEADS, V_DIM)


def mla_sample_attend(q, lat_new, kr_new, pool_lat, pool_kr, page_table, w_ukv, k_gain):
    DS = q.shape[1]
    kpos = jnp.arange(PAST_LEN + DS)
    qpos = PAST_LEN + jnp.arange(DS)
    mask = kpos[None, :] <= qpos[:, None]

    def one(args):
        q_s, lat_s, kr_s, pages = args
        lat = jnp.concatenate([pool_lat[pages].reshape(-1, KV_LORA), lat_s.astype(pool_lat.dtype)], axis=0)
        kr = jnp.concatenate([pool_kr[pages].reshape(-1, ROPE_DIM), kr_s.astype(pool_kr.dtype)], axis=0)
        k, v = mla_kv(lat[None], kr[None], w_ukv, k_gain)
        return attn_core(q_s[None], k, v, mask)[0]

    return lax.map(one, (q, lat_new, kr_new, page_table))


def memory_kv(mem, mem_norm, mem_w_kv, mem_k_gain):
    B = mem.shape[0]
    kv = rmsnorm(mem, mem_norm) @ mem_w_kv
    k = rmsnorm(kv[..., :MEM_WIDTH].reshape(B, N_MEM, MEM_HEADS, MEM_HD), mem_k_gain)
    v = kv[..., MEM_WIDTH:].reshape(B, N_MEM, MEM_HEADS, MEM_HD)
    return k, v


def token_mixing(h, pos, conv_a_state, conv_b_state, lru_h0, mem_k, mem_v, attend,
                 w_in, conv_a_w, conv_b_w, conv_b_bias, lru_gate_w, lru_gate_b, lru_lambda,
                 mla_q_norm, mla_w_uq, mla_kv_norm, mla_q_gain, mem_q_gain, w_branch, w_out):
    B, T, _ = h.shape
    idx = [int(c) for c in np.cumsum(W_IN_SPLITS)[:-1]]
    parts = jnp.split(h @ w_in, idx, axis=-1)
    a_b, a_c, a_x, r_x, r_y, c_q, c_kv, c_kr, m_q = parts[:9]
    gate_parts = parts[9:]
    u, conv_a_new = causal_dwconv(a_c * a_x, conv_a_state, conv_a_w)
    y_a = a_b * u
    xr, conv_b_new = causal_dwconv(r_x, conv_b_state, conv_b_w)
    hseq, h_last = rg_lru(xr + conv_b_bias, lru_h0, lru_gate_w, lru_gate_b, lru_lambda)
    y_r = hseq * jax.nn.gelu(r_y)
    q = (rmsnorm(c_q, mla_q_norm) @ mla_w_uq).reshape(B, T, MLA_HEADS, QK_DIM)
    q = rmsnorm(jnp.concatenate([q[..., :NOPE_DIM], rope(q[..., NOPE_DIM:], pos)], axis=-1), mla_q_gain)
    lat = rmsnorm(c_kv, mla_kv_norm)
    kr = rope(c_kr[:, :, None, :], pos)[:, :, 0]
    y_c = attend(q, lat, kr).reshape(B, T, MLA_WIDTH)
    qm = rmsnorm(m_q.reshape(B, T, MEM_HEADS, MEM_HD), mem_q_gain)
    y_m = attn_core(qm, mem_k, mem_v, None).reshape(B, T, MEM_WIDTH)
    ys = (y_a, y_r, y_c, y_m)
    offs = [int(o) for o in np.cumsum((0,) + BRANCH_WIDTHS)]
    merged = jax.nn.sigmoid(gate_parts[0]) * (ys[0] @ w_branch[offs[0]:offs[1]])
    for bi in range(1, N_BRANCH):
        merged = merged + jax.nn.sigmoid(gate_parts[bi]) * (ys[bi] @ w_branch[offs[bi]:offs[bi + 1]])
    return merged @ w_out, (lat, kr, conv_a_new, conv_b_new, h_last)


def setup_inputs(seed: int = 0) -> dict:
    key = jax.random.key(seed)
    ks = iter(list(jax.random.split(key, 48)))
    f32 = jnp.float32

    def nrm(shape, scale=1.0):
        return scale * jax.random.normal(next(ks), shape, f32)

    def gain(n):
        return 1.0 + 0.05 * nrm((DEPTH, n))

    n_pages = PAST_LEN // PAGE_SIZE
    n_used = DEC_BATCH * n_pages
    n_phys = n_used + max(1, n_used // 4)
    inp = {}
    inp['x_prompt'] = nrm((BATCH, SEQ, D_MODEL))
    inp['x_sample'] = nrm((DEC_BATCH, DEC_SEQ, D_MODEL))
    inp['cache_kv_latent'] = nrm((DEPTH, n_phys, PAGE_SIZE, KV_LORA))
    inp['cache_k_rope'] = nrm((DEPTH, n_phys, PAGE_SIZE, ROPE_DIM))
    inp['cache_mem_k'] = nrm((DEPTH, DEC_BATCH, N_MEM, MEM_HEADS, MEM_HD))
    inp['cache_mem_v'] = nrm((DEPTH, DEC_BATCH, N_MEM, MEM_HEADS, MEM_HD))
    inp['state_conv_a'] = nrm((DEPTH, DEC_BATCH, CONV_A_WIDTH - 1, D_CONV))
    inp['state_conv_b'] = nrm((DEPTH, DEC_BATCH, CONV_B_WIDTH - 1, D_LRU))
    inp['state_lru_h'] = nrm((DEPTH, DEC_BATCH, D_LRU), 0.5)
    inp['page_table'] = jax.random.permutation(next(ks), n_phys)[:n_used].reshape(DEC_BATCH, n_pages).astype(jnp.int32)
    inp['mem_prompt'] = nrm((BATCH, N_MEM, D_MODEL))
    inp['norm_ffn1'] = gain(D_MODEL)
    inp['ffn1_w_in'] = nrm((DEPTH, D_MODEL, 2 * D_FF), D_MODEL ** -0.5)
    inp['ffn1_w_out'] = nrm((DEPTH, D_FF, D_MODEL), D_FF ** -0.5)
    inp['norm_mix'] = gain(D_MODEL)
    inp['w_in'] = nrm((DEPTH, D_MODEL, W_IN_COLS), D_MODEL ** -0.5)
    inp['conv_a_w'] = nrm((DEPTH, CONV_A_WIDTH, D_CONV), CONV_A_WIDTH ** -0.5)
    inp['conv_b_w'] = nrm((DEPTH, CONV_B_WIDTH, D_LRU), CONV_B_WIDTH ** -0.5)
    inp['conv_b_bias'] = nrm((DEPTH, D_LRU), 0.01)
    inp['lru_gate_w'] = nrm((DEPTH, LRU_BLOCKS, LRU_BLK, 2 * LRU_BLK), LRU_BLK ** -0.5)
    inp['lru_gate_b'] = nrm((DEPTH, LRU_BLOCKS, 2 * LRU_BLK), 0.01)
    u = jax.random.uniform(next(ks), (DEPTH, D_LRU), f32, 0.9, 0.999)
    inp['lru_lambda'] = jnp.log(u) - jnp.log1p(-u)
    inp['mla_q_norm'] = gain(Q_LORA)
    inp['mla_w_uq'] = nrm((DEPTH, Q_LORA, MLA_HEADS * QK_DIM), Q_LORA ** -0.5)
    inp['mla_kv_norm'] = gain(KV_LORA)
    inp['mla_w_ukv'] = nrm((DEPTH, KV_LORA, MLA_HEADS * (NOPE_DIM + V_DIM)), KV_LORA ** -0.5)
    inp['mla_q_gain'] = gain(QK_DIM)
    inp['mla_k_gain'] = gain(QK_DIM)
    inp['mem_norm'] = gain(D_MODEL)
    inp['mem_w_kv'] = nrm((DEPTH, D_MODEL, 2 * MEM_WIDTH), D_MODEL ** -0.5)
    inp['mem_k_gain'] = gain(MEM_HD)
    inp['mem_q_gain'] = gain(MEM_HD)
    inp['w_branch'] = nrm((DEPTH, MIX_WIDTH, D_MODEL), 512 ** -0.5)
    inp['w_out'] = nrm((DEPTH, D_MODEL, D_MODEL), D_MODEL ** -0.5)
    inp['norm_ffn2'] = gain(D_MODEL)
    inp['ffn2_w_in'] = nrm((DEPTH, D_MODEL, 2 * D_FF), D_MODEL ** -0.5)
    inp['ffn2_w_out'] = nrm((DEPTH, D_FF, D_MODEL), D_FF ** -0.5)
    return inp


def reference(x_prompt, x_sample, cache_kv_latent, cache_k_rope, cache_mem_k, cache_mem_v,
              state_conv_a, state_conv_b, state_lru_h, page_table, mem_prompt,
              norm_ffn1, ffn1_w_in, ffn1_w_out, norm_mix, w_in, conv_a_w, conv_b_w, conv_b_bias,
              lru_gate_w, lru_gate_b, lru_lambda, mla_q_norm, mla_w_uq, mla_kv_norm, mla_w_ukv,
              mla_q_gain, mla_k_gain, mem_norm, mem_w_kv, mem_k_gain, mem_q_gain, w_branch, w_out,
              norm_ffn2, ffn2_w_in, ffn2_w_out):
    B, S = x_prompt.shape[0], x_prompt.shape[1]
    DB, DS = x_sample.shape[0], x_sample.shape[1]
    pos_p = jnp.arange(S)
    pos_s = PAST_LEN + jnp.arange(DS)
    xp, xs = x_prompt, x_sample
    p_lat, p_kr, p_ca, p_cb, p_h, p_mk, p_mv = [], [], [], [], [], [], []
    s_lat, s_kr, s_ca, s_cb, s_h = [], [], [], [], []
    for l in range(DEPTH):
        def layer(x, pos, ca, cb, h0, mk, mv, attend):
            x = x + 0.5 * swiglu(rmsnorm(x, norm_ffn1[l]), ffn1_w_in[l], ffn1_w_out[l])
            mix, new = token_mixing(rmsnorm(x, norm_mix[l]), pos, ca, cb, h0, mk, mv, attend,
                                    w_in[l], conv_a_w[l], conv_b_w[l], conv_b_bias[l], lru_gate_w[l],
                                    lru_gate_b[l], lru_lambda[l], mla_q_norm[l], mla_w_uq[l],
                                    mla_kv_norm[l], mla_q_gain[l], mem_q_gain[l], w_branch[l], w_out[l])
            x = x + mix
            x = x + 0.5 * swiglu(rmsnorm(x, norm_ffn2[l]), ffn2_w_in[l], ffn2_w_out[l])
            return x, new

        mk_p, mv_p = memory_kv(mem_prompt, mem_norm[l], mem_w_kv[l], mem_k_gain[l])
        attend_p = lambda q, lat, kr: mla_prompt_attend(q, lat, kr, mla_w_ukv[l], mla_k_gain[l])
        xp, (lat, kr, ca, cb, hl) = layer(
            xp, pos_p,
            jnp.zeros((B, CONV_A_WIDTH - 1, D_CONV), xp.dtype),
            jnp.zeros((B, CONV_B_WIDTH - 1, D_LRU), xp.dtype),
            jnp.zeros((B, D_LRU), xp.dtype), mk_p, mv_p, attend_p)
        p_lat.append(lat); p_kr.append(kr); p_ca.append(ca); p_cb.append(cb); p_h.append(hl)
        p_mk.append(mk_p); p_mv.append(mv_p)

        attend_s = lambda q, lat, kr: mla_sample_attend(q, lat, kr, cache_kv_latent[l], cache_k_rope[l],
                                                        page_table, mla_w_ukv[l], mla_k_gain[l])
        xs, (lat, kr, ca, cb, hl) = layer(xs, pos_s, state_conv_a[l], state_conv_b[l], state_lru_h[l],
                                          cache_mem_k[l], cache_mem_v[l], attend_s)
        s_lat.append(lat); s_kr.append(kr); s_ca.append(ca); s_cb.append(cb); s_h.append(hl)

    y_prompt, y_sample = xp, xs
    p_kv_latent = jnp.stack(p_lat)
    p_k_rope = jnp.stack(p_kr)
    p_conv_a = jnp.stack(p_ca)
    p_conv_b = jnp.stack(p_cb)
    p_lru_h = jnp.stack(p_h)
    p_mem_k = jnp.stack(p_mk)
    p_mem_v = jnp.stack(p_mv)
    s_kv_latent = jnp.stack(s_lat)
    s_k_rope = jnp.stack(s_kr)
    s_conv_a = jnp.stack(s_ca)
    s_conv_b = jnp.stack(s_cb)
    s_lru_h = jnp.stack(s_h)
    return (y_prompt, y_sample, p_kv_latent, p_k_rope, p_conv_a, p_conv_b, p_lru_h, p_mem_k, p_mem_v,
            s_kv_latent, s_k_rope, s_conv_a, s_conv_b, s_lru_h)
```

```python
import functools

import numpy as np
import jax
import jax.numpy as jnp
from jax import lax
from jax.experimental import pallas as pl
from jax.experimental.pallas import tpu as pltpu

F32 = jnp.float32
BF16 = jnp.bfloat16

D_FF = 2816
D_CONV = 512
D_LRU = 512
LRU_BLOCKS = 8
LRU_C = 8.0
MLA_HEADS = 8
Q_LORA = 384
KV_LORA = 256
NOPE_DIM = 64
ROPE_DIM = 32
V_DIM = 64
QK_DIM = NOPE_DIM + ROPE_DIM
ROPE_THETA = 10000.0
N_MEM = 256
MEM_HEADS = 4
MEM_HD = 128
MEM_WIDTH = MEM_HEADS * MEM_HD
N_BRANCH = 4
EPS = 1e-6
NEG = -1e30

LANES = 128
SUBLANES = 8
HEAD_SLOT = LANES
VMEM_LIMIT = 56 * 1024 * 1024

P_GATES = 0
P_CONV = 4096
P_TAIL = 6144
P_WIDTH = 8192
P_CHUNK = 2048
T_RY, T_CQ, T_CKV, T_KRA, T_KRB, T_MQ = 0, 512, 896, 1152, 1280, 1408


def _dot(a, b):
    return jnp.dot(a, b, preferred_element_type=F32)


def _dot_nt(a, b):
    return lax.dot_general(a, b, (((1,), (1,)), ((), ())), preferred_element_type=F32)


def _rms(x, g):
    return x * lax.rsqrt(jnp.mean(x * x, axis=-1, keepdims=True) + EPS) * g


def _params(*sem):
    return pltpu.CompilerParams(dimension_semantics=sem, vmem_limit_bytes=VMEM_LIMIT)


def _ffn_kernel(x_ref, g_ref, wg_ref, wu_ref, wd_ref, o_ref, xn_ref, acc_ref, *, nf):
    j = pl.program_id(1)

    @pl.when(j == 0)
    def _():
        xn_ref[...] = _rms(x_ref[...], g_ref[...]).astype(BF16)
        acc_ref[...] = jnp.zeros_like(acc_ref)

    xn = xn_ref[...]
    g = _dot(xn, wg_ref[...])
    u = _dot(xn, wu_ref[...])
    h = (g * jax.nn.sigmoid(g) * u).astype(BF16)
    acc_ref[...] += _dot(h, wd_ref[...])

    @pl.when(j == nf - 1)
    def _():
        o_ref[...] = x_ref[...] + 0.5 * acc_ref[...]


def _ffn(x, gain, wg, wu, wd, l, tm):
    T, D = x.shape
    nf = 2
    tf = wg.shape[2] // nf
    return pl.pallas_call(
        functools.partial(_ffn_kernel, nf=nf),
        grid=(T // tm, nf),
        in_specs=[
            pl.BlockSpec((tm, D), lambda i, j: (i, 0)),
            pl.BlockSpec((None, 1, D), lambda i, j: (l, 0, 0)),
            pl.BlockSpec((None, D, tf), lambda i, j: (l, 0, j)),
            pl.BlockSpec((None, D, tf), lambda i, j: (l, 0, j)),
            pl.BlockSpec((None, tf, D), lambda i, j: (l, j, 0)),
        ],
        out_specs=pl.BlockSpec((tm, D), lambda i, j: (i, 0)),
        out_shape=jax.ShapeDtypeStruct((T, D), F32),
        scratch_shapes=[pltpu.VMEM((tm, D), BF16), pltpu.VMEM((tm, D), F32)],
        compiler_params=_params("parallel", "arbitrary"),
        name="ffn",
    )(x, gain, wg, wu, wd)


def _proj_kernel(x_ref, g_ref, w_ref, o_ref, xn_ref, *, n_gate_chunks):
    j = pl.program_id(1)

    @pl.when(j == 0)
    def _():
        xn_ref[...] = _rms(x_ref[...], g_ref[...]).astype(BF16)

    y = _dot(xn_ref[...], w_ref[...])

    @pl.when(j < n_gate_chunks)
    def _():
        o_ref[...] = jax.nn.sigmoid(y)

    @pl.when(j >= n_gate_chunks)
    def _():
        o_ref[...] = y


def _proj(x, gain, w, l, tm):
    T, D = x.shape
    return pl.pallas_call(
        functools.partial(_proj_kernel, n_gate_chunks=P_CONV // P_CHUNK),
        grid=(T // tm, P_WIDTH // P_CHUNK),
        in_specs=[
            pl.BlockSpec((tm, D), lambda i, j: (i, 0)),
            pl.BlockSpec((None, 1, D), lambda i, j: (l, 0, 0)),
            pl.BlockSpec((None, D, P_CHUNK), lambda i, j: (l, 0, j)),
        ],
        out_specs=pl.BlockSpec((tm, P_CHUNK), lambda i, j: (i, j)),
        out_shape=jax.ShapeDtypeStruct((T, P_WIDTH), F32),
        scratch_shapes=[pltpu.VMEM((tm, D), BF16)],
        compiler_params=_params("parallel", "arbitrary"),
        name="proj",
    )(x, gain, w)


def _head_norm_store(dst_ref, x, gain, n_heads, inv_width):
    for h in range(n_heads):
        sl = slice(h * HEAD_SLOT, (h + 1) * HEAD_SLOT)
        xh = x[:, sl]
        ms = jnp.sum(xh * xh, axis=-1, keepdims=True) * inv_width
        dst_ref[:, sl] = (xh * lax.rsqrt(ms + EPS) * gain).astype(dst_ref.dtype)


def _prep_kernel(p_ref, cos_ref, sin_ref, qn_ref, wa_ref, wb_ref, qg_ref, kvn_ref, wk_ref, wv_ref,
                 kg_ref, mqg_ref, q_ref, lat_ref, kr_ref, k_ref, v_ref, qm_ref):
    cos = cos_ref[...]
    sin = sin_ref[...]
    n = _rms(p_ref[:, T_CQ:T_CQ + Q_LORA], qn_ref[...]).astype(BF16)
    qa = _dot(n, wa_ref[...])
    qb = _dot(n, wb_ref[...])
    qg = qg_ref[...] * (QK_DIM ** -0.5)
    for h in range(MLA_HEADS):
        sl = slice(h * HEAD_SLOT, (h + 1) * HEAD_SLOT)
        qh = qa[:, sl] * cos + qb[:, sl] * sin
        ms = jnp.sum(qh * qh, axis=-1, keepdims=True) * (1.0 / QK_DIM)
        q_ref[:, sl] = (qh * lax.rsqrt(ms + EPS) * qg).astype(q_ref.dtype)
    lat = _rms(p_ref[:, T_CKV:T_CKV + KV_LORA], kvn_ref[...])
    lat_ref[...] = lat
    kr = p_ref[:, T_KRA:T_KRA + LANES] * cos + p_ref[:, T_KRB:T_KRB + LANES] * sin
    kr_ref[...] = kr
    latb = lat.astype(BF16)
    kn = _dot(latb, wk_ref[...])
    kg = kg_ref[...]
    for h in range(MLA_HEADS):
        sl = slice(h * HEAD_SLOT, (h + 1) * HEAD_SLOT)
        kh = kn[:, sl] + kr
        ms = jnp.sum(kh * kh, axis=-1, keepdims=True) * (1.0 / QK_DIM)
        k_ref[:, sl] = (kh * lax.rsqrt(ms + EPS) * kg).astype(k_ref.dtype)
    v_ref[...] = _dot(latb, wv_ref[...]).astype(v_ref.dtype)
    _head_norm_store(qm_ref, p_ref[:, T_MQ:T_MQ + MEM_WIDTH], mqg_ref[...] * (MEM_HD ** -0.5),
                     MEM_HEADS, 1.0 / MEM_HD)


def _prep(P, cos, sin, w, l, tm):
    T = P.shape[0]
    HW = MLA_HEADS * HEAD_SLOT
    lay = lambda *shape: pl.BlockSpec((None,) + shape, lambda i: (l,) + (0,) * len(shape))
    row = lambda width: pl.BlockSpec((tm, width), lambda i: (i, 0))
    return pl.pallas_call(
        _prep_kernel,
        grid=(T // tm,),
        in_specs=[
            pl.BlockSpec((tm, P_CHUNK), lambda i: (i, P_TAIL // P_CHUNK)),
            row(LANES), row(LANES),
            lay(1, Q_LORA), lay(Q_LORA, HW), lay(Q_LORA, HW), lay(1, HEAD_SLOT),
            lay(1, KV_LORA), lay(KV_LORA, HW), lay(KV_LORA, MLA_HEADS * V_DIM), lay(1, HEAD_SLOT),
            lay(1, MEM_HD),
        ],
        out_specs=[row(HW), row(KV_LORA), row(LANES), row(HW), row(MLA_HEADS * V_DIM), row(MEM_WIDTH)],
        out_shape=[
            jax.ShapeDtypeStruct((T, HW), BF16),
            jax.ShapeDtypeStruct((T, KV_LORA), F32),
            jax.ShapeDtypeStruct((T, LANES), F32),
            jax.ShapeDtypeStruct((T, HW), BF16),
            jax.ShapeDtypeStruct((T, MLA_HEADS * V_DIM), BF16),
            jax.ShapeDtypeStruct((T, MEM_WIDTH), BF16),
        ],
        compiler_params=_params("parallel"),
        name="prep",
    )(P, cos, sin, w["q_norm"], w["uq_a"], w["uq_b"], w["q_gain"], w["kv_norm"], w["wk_slot"],
      w["wv"], w["k_gain"], w["mq_gain"])


def _lru_coeffs(xr, wr, wi, br, bi, lam):
    xb = xr.astype(BF16)
    r = jax.nn.sigmoid(_dot(xb, wr) + br)
    i = jax.nn.sigmoid(_dot(xb, wi) + bi)
    log_a = -LRU_C * r * jax.nn.softplus(-lam)
    a = jnp.exp(log_a)
    b = jnp.sqrt(1.0 - a * a) * (i * xr)
    return a, b


def _scan(a, b, pos, seg):
    d = 1
    while d < seg:
        keep = pos >= d
        a_s = jnp.where(keep, pltpu.roll(a, d, 0), 1.0)
        b_s = jnp.where(keep, pltpu.roll(b, d, 0), 0.0)
        b = a * b_s + b
        a = a * a_s
        d *= 2
    return a, b


def _conv_taps(x, shifted, w, bias=None):
    K = w.shape[0]
    y = shifted(K - 1) * w[0:1]
    for k in range(1, K - 1):
        y = y + shifted(K - 1 - k) * w[k:k + 1]
    y = y + x * w[K - 1:K]
    if bias is not None:
        y = y + bias
    return y


def _convlru_prompt_kernel(p_ref, ry_ref, caw_ref, cbw_ref, cbb_ref, wr_ref, wi_ref, br_ref, bi_ref,
                           lam_ref, ya_ref, yr_ref, ta_ref, tb_ref, th_ref, pa_ref, pb_ref, h_ref):
    tt = p_ref.shape[0]

    @pl.when(pl.program_id(1) == 0)
    def _():
        pa_ref[...] = jnp.zeros_like(pa_ref)
        pb_ref[...] = jnp.zeros_like(pb_ref)
        h_ref[...] = jnp.zeros_like(h_ref)

    row = lax.broadcasted_iota(jnp.int32, (tt, 1), 0)
    row8 = lax.broadcasted_iota(jnp.int32, (SUBLANES, 1), 0)

    def delayed(x, prev8):
        def shifted(s):
            r = pltpu.roll(x, s, 0)
            first = jnp.where(row8 < s, pltpu.roll(prev8, s, 0), r[:SUBLANES])
            return jnp.concatenate([first, r[SUBLANES:]], axis=0)
        return shifted

    a_b = p_ref[:, 0:D_CONV]
    v = p_ref[:, D_CONV:2 * D_CONV] * p_ref[:, 2 * D_CONV:3 * D_CONV]
    u = _conv_taps(v, delayed(v, pa_ref[...]), caw_ref[...])
    ya_ref[...] = (a_b * u).astype(ya_ref.dtype)

    rx = p_ref[:, 3 * D_CONV:3 * D_CONV + D_LRU]
    xr = _conv_taps(rx, delayed(rx, pb_ref[...]), cbw_ref[...], cbb_ref[...])
    a, b = _lru_coeffs(xr, wr_ref[...], wi_ref[...], br_ref[...], bi_ref[...], lam_ref[...])
    a, b = _scan(a, b, row, tt)
    h = b + a * h_ref[...]
    yr_ref[...] = (h * jax.nn.gelu(ry_ref[...])).astype(yr_ref.dtype)

    tail = slice(tt - SUBLANES, tt)
    ta_ref[...] = v[tail]
    tb_ref[...] = rx[tail]
    th_ref[...] = h[tail]
    pa_ref[...] = v[tail]
    pb_ref[...] = rx[tail]
    h_ref[...] = h[tt - 1:tt]


def _convlru_prompt(P, w, l, B, S, tt):
    nt = S // tt
    cb = P_CHUNK // D_CONV
    lay = lambda *shape: pl.BlockSpec((None,) + shape, lambda b, t: (l,) + (0,) * len(shape))
    seq = pl.BlockSpec((tt, D_CONV), lambda b, t: (b * nt + t, 0))
    tail = pl.BlockSpec((None, SUBLANES, D_CONV), lambda b, t: (b, 0, 0))
    tail_shape = jax.ShapeDtypeStruct((B, SUBLANES, D_CONV), F32)
    return pl.pallas_call(
        _convlru_prompt_kernel,
        grid=(B, nt),
        in_specs=[
            pl.BlockSpec((tt, P_CHUNK), lambda b, t: (b * nt + t, P_CONV // P_CHUNK)),
            pl.BlockSpec((tt, D_LRU), lambda b, t: (b * nt + t, P_TAIL // D_LRU)),
            lay(3, D_CONV), lay(4, D_LRU), lay(1, D_LRU), lay(D_LRU, D_LRU), lay(D_LRU, D_LRU),
            lay(1, D_LRU), lay(1, D_LRU), lay(1, D_LRU),
        ],
        out_specs=[seq, seq, tail, tail, tail],
        out_shape=[
            jax.ShapeDtypeStruct((B * S, D_CONV), BF16),
            jax.ShapeDtypeStruct((B * S, D_LRU), BF16),
            tail_shape, tail_shape, tail_shape,
        ],
        scratch_shapes=[pltpu.VMEM((SUBLANES, D_CONV), F32), pltpu.VMEM((SUBLANES, D_LRU), F32),
                        pltpu.VMEM((1, D_LRU), F32)],
        compiler_params=_params("parallel", "arbitrary"),
        name="convlru_prompt",
    )(P, P, w["conv_a_w"], w["conv_b_w"], w["conv_b_bias"], w["lru_wr"], w["lru_wi"], w["lru_br"],
      w["lru_bi"], w["lru_lambda"])


def _convlru_sample_kernel(p_ref, ry_ref, sa_ref, sb_ref, h0_ref, caw_ref, cbw_ref, cbb_ref, wr_ref,
                           wi_ref, br_ref, bi_ref, lam_ref, ya_ref, yr_ref, va_ref, h_ref, *, ds):
    ts = p_ref.shape[0]
    pos = lax.broadcasted_iota(jnp.int32, (ts, 1), 0) % ds

    def delayed(x, state):
        def shifted(s):
            return jnp.where(pos < s, pltpu.roll(state, ts - ds + s, 0), pltpu.roll(x, s, 0))
        return shifted

    a_b = p_ref[:, 0:D_CONV]
    v = p_ref[:, D_CONV:2 * D_CONV] * p_ref[:, 2 * D_CONV:3 * D_CONV]
    u = _conv_taps(v, delayed(v, sa_ref[...]), caw_ref[...])
    ya_ref[...] = (a_b * u).astype(ya_ref.dtype)
    va_ref[...] = v

    rx = p_ref[:, 3 * D_CONV:3 * D_CONV + D_LRU]
    xr = _conv_taps(rx, delayed(rx, sb_ref[...]), cbw_ref[...], cbb_ref[...])
    a, b = _lru_coeffs(xr, wr_ref[...], wi_ref[...], br_ref[...], bi_ref[...], lam_ref[...])
    b = b + a * h0_ref[...]
    _, h = _scan(a, b, pos, ds)
    h_ref[...] = h
    yr_ref[...] = (h * jax.nn.gelu(ry_ref[...])).astype(yr_ref.dtype)


def _convlru_sample(P, sa, sb, h0, w, l, row0, Ts, ds, ts):
    assert ds == SUBLANES
    r0 = row0 // ts
    lay = lambda *shape: pl.BlockSpec((None,) + shape, lambda i: (l,) + (0,) * len(shape))
    st = pl.BlockSpec((None, ts, D_CONV), lambda i: (l, i, 0))
    seq = pl.BlockSpec((ts, D_CONV), lambda i: (i, 0))
    return pl.pallas_call(
        functools.partial(_convlru_sample_kernel, ds=ds),
        grid=(Ts // ts,),
        in_specs=[
            pl.BlockSpec((ts, P_CHUNK), lambda i: (r0 + i, P_CONV // P_CHUNK)),
            pl.BlockSpec((ts, D_LRU), lambda i: (r0 + i, P_TAIL // D_LRU)),
            st, st, st,
            lay(3, D_CONV), lay(4, D_LRU), lay(1, D_LRU), lay(D_LRU, D_LRU), lay(D_LRU, D_LRU),
            lay(1, D_LRU), lay(1, D_LRU), lay(1, D_LRU),
        ],
        out_specs=[seq, seq, seq, seq],
        out_shape=[
            jax.ShapeDtypeStruct((Ts, D_CONV), BF16),
            jax.ShapeDtypeStruct((Ts, D_LRU), BF16),
            jax.ShapeDtypeStruct((Ts, D_CONV), F32),
            jax.ShapeDtypeStruct((Ts, D_LRU), F32),
        ],
        compiler_params=_params("parallel"),
        name="convlru_sample",
    )(P, P, sa, sb, h0, w["conv_a_w"], w["conv_b_w"], w["conv_b_bias"], w["lru_wr"], w["lru_wi"],
      w["lru_br"], w["lru_bi"], w["lru_lambda"])


def _flash_kernel(qi_ref, ki_ref, q_ref, k_ref, v_ref, o_ref, m_ref, l_ref, acc_ref, *, tq, tk):
    p = pl.program_id(2)
    qi = qi_ref[p]
    ki = ki_ref[p]

    @pl.when(ki == 0)
    def _():
        m_ref[...] = jnp.full_like(m_ref, NEG)
        l_ref[...] = jnp.zeros_like(l_ref)
        acc_ref[...] = jnp.zeros_like(acc_ref)

    qpos = qi * tq + lax.broadcasted_iota(jnp.int32, (tq, 1), 0)
    kpos = ki * tk + lax.broadcasted_iota(jnp.int32, (1, tk), 1)
    mask = kpos <= qpos
    first_head = lax.broadcasted_iota(jnp.int32, (1, LANES), 1) < V_DIM
    v = v_ref[...]
    alpha, pv = [], []
    for hh in range(2):
        sl = slice(hh * HEAD_SLOT, (hh + 1) * HEAD_SLOT)
        s = jnp.where(mask, _dot_nt(q_ref[:, sl], k_ref[:, sl]), NEG)
        m_prev = m_ref[hh]
        m_new = jnp.maximum(m_prev, jnp.max(s, axis=-1, keepdims=True))
        al = jnp.exp(m_prev - m_new)
        pr = jnp.exp(s - m_new)
        l_ref[hh] = al * l_ref[hh] + jnp.sum(pr, axis=-1, keepdims=True)
        m_ref[hh] = m_new
        alpha.append(al)
        pv.append(_dot(pr.astype(BF16), v))
    acc_ref[...] = (jnp.where(first_head, alpha[0], alpha[1]) * acc_ref[...]
                    + jnp.where(first_head, pv[0], pv[1]))

    @pl.when(ki == ((qi + 1) * tq - 1) // tk)
    def _():
        o_ref[...] = (acc_ref[...] / jnp.where(first_head, l_ref[0], l_ref[1])).astype(o_ref.dtype)


def _flash(q, k, v, B, S, tq, tk):
    nq, nk = S // tq, S // tk
    pairs = [(i, j) for i in range(nq) for j in range(((i + 1) * tq - 1) // tk + 1)]
    qi = jnp.asarray(np.array([p[0] for p in pairs], np.int32))
    ki = jnp.asarray(np.array([p[1] for p in pairs], np.int32))
    hp = MLA_HEADS // 2
    grid_spec = pltpu.PrefetchScalarGridSpec(
        num_scalar_prefetch=2,
        grid=(B, hp, len(pairs)),
        in_specs=[
            pl.BlockSpec((tq, 2 * HEAD_SLOT), lambda b, h, p, qi, ki: (b * nq + qi[p], h)),
            pl.BlockSpec((tk, 2 * HEAD_SLOT), lambda b, h, p, qi, ki: (b * nk + ki[p], h)),
            pl.BlockSpec((tk, 2 * V_DIM), lambda b, h, p, qi, ki: (b * nk + ki[p], h)),
        ],
        out_specs=pl.BlockSpec((tq, 2 * V_DIM), lambda b, h, p, qi, ki: (b * nq + qi[p], h)),
        scratch_shapes=[pltpu.VMEM((2, tq, 1), F32), pltpu.VMEM((2, tq, 1), F32),
                        pltpu.VMEM((tq, 2 * V_DIM), F32)],
    )
    return pl.pallas_call(
        functools.partial(_flash_kernel, tq=tq, tk=tk),
        grid_spec=grid_spec,
        out_shape=jax.ShapeDtypeStruct((B * S, MLA_HEADS * V_DIM), BF16),
        compiler_params=_params("parallel", "parallel", "arbitrary"),
        name="flash",
    )(qi, ki, q, k, v)


def _sattn_kernel(pt_ref, q_ref, latn_ref, krn_ref, *rest, G, page, ds):
    lat_refs = rest[:G]
    kr_refs = rest[G:2 * G]
    (wk_ref, wkt_ref, wv_ref, kg_ref, o_ref,
     qabs_ref, qpe_ref, m_ref, l_ref, acc_ref, lbuf_ref, rbuf_ref) = rest[2 * G:]
    g = pl.program_id(1)
    R = MLA_HEADS * ds

    hrow = lax.broadcasted_iota(jnp.int32, (R, MLA_HEADS * NOPE_DIM), 0) // ds
    hcol = lax.broadcasted_iota(jnp.int32, (R, MLA_HEADS * NOPE_DIM), 1) // NOPE_DIM
    head_sel = (hrow == hcol).astype(BF16)
    ones_pe = jnp.ones((R, ROPE_DIM), BF16)

    def update(L, Rk, mask):
        kn = _dot(L, wk_ref[...])
        ssq = _dot_nt(head_sel, (kn * kn).astype(BF16)) + _dot_nt(ones_pe, (Rk * Rk).astype(BF16))
        rinv = lax.rsqrt(ssq * (1.0 / QK_DIM) + EPS)
        s = (_dot_nt(qabs_ref[...], L) + _dot_nt(qpe_ref[...], Rk.astype(BF16))) * rinv
        if mask is not None:
            s = jnp.where(mask, s, NEG)
        m_prev = m_ref[...]
        m_new = jnp.maximum(m_prev, jnp.max(s, axis=-1, keepdims=True))
        al = jnp.exp(m_prev - m_new)
        pr = jnp.exp(s - m_new)
        l_ref[...] = al * l_ref[...] + jnp.sum(pr, axis=-1, keepdims=True)
        m_ref[...] = m_new
        acc_ref[...] = al * acc_ref[...] + _dot(pr.astype(BF16), L)

    @pl.when(g == 0)
    def _():
        qk = q_ref[...] * kg_ref[...]
        qa, qp = [], []
        for h in range(MLA_HEADS):
            qh = qk[:, h * HEAD_SLOT:(h + 1) * HEAD_SLOT]
            qa.append(_dot(qh.astype(BF16), wkt_ref[h]))
            qp.append(qh[:, 0:ROPE_DIM])
        qabs_ref[...] = jnp.concatenate(qa, axis=0).astype(BF16)
        qpe_ref[...] = jnp.concatenate(qp, axis=0).astype(BF16)
        m_ref[...] = jnp.full_like(m_ref, NEG)
        l_ref[...] = jnp.zeros_like(l_ref)
        acc_ref[...] = jnp.zeros_like(acc_ref)
        pad = jnp.zeros((page - ds, KV_LORA), F32)
        Ln = jnp.concatenate([latn_ref[...], pad], axis=0).astype(BF16)
        Rn = jnp.concatenate([krn_ref[:, 0:ROPE_DIM], jnp.zeros((page - ds, ROPE_DIM), F32)], axis=0)
        key = lax.broadcasted_iota(jnp.int32, (R, page), 1)
        qry = lax.broadcasted_iota(jnp.int32, (R, page), 0) % ds
        update(Ln, Rn, key <= qry)

    for i in range(G):
        lbuf_ref[i * page:(i + 1) * page, :] = lat_refs[i][...].astype(BF16)
        rbuf_ref[i * page:(i + 1) * page, :] = kr_refs[i][...]
    update(lbuf_ref[...], rbuf_ref[...], None)

    @pl.when(g == pl.num_programs(1) - 1)
    def _():
        o_lat = (acc_ref[...] / l_ref[...]).astype(BF16)
        y = _dot(o_lat, wv_ref[...])
        vrow = lax.broadcasted_iota(jnp.int32, (R, MLA_HEADS * V_DIM), 0) // ds
        vcol = lax.broadcasted_iota(jnp.int32, (R, MLA_HEADS * V_DIM), 1) // V_DIM
        y = jnp.where(vrow == vcol, y, 0.0)
        o_ref[...] = jnp.sum(y.reshape(MLA_HEADS, ds, MLA_HEADS * V_DIM), axis=0)


def _sattn(page_table, q_s, lat, kr, cache_lat, cache_kr, w, l, row0, DB, ds, G):
    n_pages = page_table.shape[1]
    page = cache_lat.shape[2]
    assert n_pages % G == 0 and ds == SUBLANES
    HW = MLA_HEADS * HEAD_SLOT
    R = MLA_HEADS * ds
    r0 = row0 // ds
    lay = lambda *shape: pl.BlockSpec((None,) + shape, lambda b, g, pt: (l,) + (0,) * len(shape))

    def page_spec(width, i):
        return pl.BlockSpec((None, None, page, width), lambda b, g, pt: (l, pt[b, g * G + i], 0, 0))

    grid_spec = pltpu.PrefetchScalarGridSpec(
        num_scalar_prefetch=1,
        grid=(DB, n_pages // G),
        in_specs=[
            pl.BlockSpec((ds, HW), lambda b, g, pt: (b, 0)),
            pl.BlockSpec((ds, KV_LORA), lambda b, g, pt: (r0 + b, 0)),
            pl.BlockSpec((ds, LANES), lambda b, g, pt: (r0 + b, 0)),
        ] + [page_spec(KV_LORA, i) for i in range(G)] + [page_spec(ROPE_DIM, i) for i in range(G)] + [
            lay(KV_LORA, MLA_HEADS * NOPE_DIM), lay(MLA_HEADS, HEAD_SLOT, KV_LORA),
            lay(KV_LORA, MLA_HEADS * V_DIM), lay(1, HW),
        ],
        out_specs=pl.BlockSpec((ds, MLA_HEADS * V_DIM), lambda b, g, pt: (b, 0)),
        scratch_shapes=[
            pltpu.VMEM((R, KV_LORA), BF16), pltpu.VMEM((R, ROPE_DIM), BF16),
            pltpu.VMEM((R, 1), F32), pltpu.VMEM((R, 1), F32), pltpu.VMEM((R, KV_LORA), F32),
            pltpu.VMEM((G * page, KV_LORA), BF16), pltpu.VMEM((G * page, ROPE_DIM), F32),
        ],
    )
    return pl.pallas_call(
        functools.partial(_sattn_kernel, G=G, page=page, ds=ds),
        grid_spec=grid_spec,
        out_shape=jax.ShapeDtypeStruct((DB * ds, MLA_HEADS * V_DIM), F32),
        compiler_params=_params("parallel", "arbitrary"),
        name="sattn",
    )(page_table, q_s, lat, kr, *([cache_lat] * G), *([cache_kr] * G),
      w["wk_all"], w["wk_t"], w["wv"], w["k_gain_tiled"])


def _softmax_rows(s):
    m = jnp.max(s, axis=-1, keepdims=True)
    p = jnp.exp(s - m)
    return p / jnp.sum(p, axis=-1, keepdims=True)


def _mem_prompt_kernel(q_ref, k_ref, v_ref, o_ref):
    for h in range(MEM_HEADS):
        sl = slice(h * MEM_HD, (h + 1) * MEM_HD)
        p = _softmax_rows(_dot_nt(q_ref[:, sl], k_ref[:, sl].astype(BF16)))
        o_ref[:, sl] = _dot(p.astype(BF16), v_ref[:, sl].astype(BF16)).astype(o_ref.dtype)


def _mem_prompt(qm, mk, mv, l, B, S, tm):
    nt = S // tm
    kv = pl.BlockSpec((None, None, N_MEM, MEM_WIDTH), lambda b, t: (l, b, 0, 0))
    seq = pl.BlockSpec((tm, MEM_WIDTH), lambda b, t: (b * nt + t, 0))
    return pl.pallas_call(
        _mem_prompt_kernel,
        grid=(B, nt),
        in_specs=[seq, kv, kv],
        out_specs=seq,
        out_shape=jax.ShapeDtypeStruct((B * S, MEM_WIDTH), BF16),
        compiler_params=_params("parallel", "parallel"),
        name="mem_prompt",
    )(qm, mk, mv)


def _mem_sample_kernel(q_ref, k_ref, v_ref, o_ref, *, nb, ds):
    q = q_ref[...].reshape(nb, ds, MEM_WIDTH)
    outs = []
    for h in range(MEM_HEADS):
        sl = slice(h * MEM_HD, (h + 1) * MEM_HD)
        s = jnp.einsum('bqd,bkd->bqk', q[:, :, sl], k_ref[:, :, sl].astype(BF16),
                       preferred_element_type=F32)
        p = _softmax_rows(s)
        outs.append(jnp.einsum('bqk,bkd->bqd', p.astype(BF16), v_ref[:, :, sl].astype(BF16),
                               preferred_element_type=F32))
    o_ref[...] = jnp.concatenate(outs, axis=-1).reshape(nb * ds, MEM_WIDTH)


def _mem_sample(qm_s, ck, cv, l, DB, ds, nb):
    kv = pl.BlockSpec((None, nb, N_MEM, MEM_WIDTH), lambda i: (l, i, 0, 0))
    seq = pl.BlockSpec((nb * ds, MEM_WIDTH), lambda i: (i, 0))
    return pl.pallas_call(
        functools.partial(_mem_sample_kernel, nb=nb, ds=ds),
        grid=(DB // nb,),
        in_specs=[seq, kv, kv],
        out_specs=seq,
        out_shape=jax.ShapeDtypeStruct((DB * ds, MEM_WIDTH), F32),
        compiler_params=_params("parallel"),
        name="mem_sample",
    )(qm_s, ck, cv)


def _memkv_kernel(x_ref, n_ref, w_ref, kg_ref, k_ref, v_ref):
    kv = _dot(_rms(x_ref[...], n_ref[...]).astype(BF16), w_ref[...])
    _head_norm_store(k_ref, kv[:, :MEM_WIDTH], kg_ref[...], MEM_HEADS, 1.0 / MEM_HD)
    v_ref[...] = kv[:, MEM_WIDTH:]


def _memkv(mem, mem_norm, w_kv, k_gain):
    depth = w_kv.shape[0]
    R, D = mem.shape
    lay = lambda *shape: pl.BlockSpec((None,) + shape, lambda l: (l,) + (0,) * len(shape))
    out = jax.ShapeDtypeStruct((depth, R, MEM_WIDTH), F32)
    return pl.pallas_call(
        _memkv_kernel,
        grid=(depth,),
        in_specs=[pl.BlockSpec((R, D), lambda l: (0, 0)), lay(1, D), lay(D, 2 * MEM_WIDTH), lay(1, MEM_HD)],
        out_specs=[lay(R, MEM_WIDTH), lay(R, MEM_WIDTH)],
        out_shape=[out, out],
        compiler_params=_params("parallel"),
        name="memkv",
    )(mem, mem_norm, w_kv, k_gain)


def _merge_kernel(x_ref, sg_ref, ya_ref, yr_ref, yc_ref, ym_ref, wb_ref, wo_ref, o_ref):
    D = x_ref.shape[1]
    merged = None
    for bi, y_ref in enumerate((ya_ref, yr_ref, yc_ref, ym_ref)):
        width = y_ref.shape[1]
        t = sg_ref[:, bi * D:(bi + 1) * D] * _dot(y_ref[...], wb_ref[bi * width:(bi + 1) * width, :])
        merged = t if merged is None else merged + t
    o_ref[...] = x_ref[...] + _dot(merged.astype(BF16), wo_ref[...])


def _merge(x, P, ya, yr, yc, ym, wb, wo, l, tm):
    T, D = x.shape
    row = lambda width: pl.BlockSpec((tm, width), lambda i: (i, 0))
    lay = lambda *shape: pl.BlockSpec((None,) + shape, lambda i: (l,) + (0,) * len(shape))
    return pl.pallas_call(
        _merge_kernel,
        grid=(T // tm,),
        in_specs=[row(D), row(N_BRANCH * D), row(D_CONV), row(D_LRU), row(MLA_HEADS * V_DIM),
                  row(MEM_WIDTH), lay(wb.shape[1], D), lay(D, D)],
        out_specs=row(D),
        out_shape=jax.ShapeDtypeStruct((T, D), F32),
        compiler_params=_params("parallel"),
        name="merge",
    )(x, P, ya, yr, yc, ym, wb, wo)


def _prepare_weights(a):
    depth, D = a["norm_ffn1"].shape
    bf = lambda t: t.astype(BF16)
    vec = lambda t: t.reshape(depth, 1, -1)
    w = {}
    for name in ("ffn1", "ffn2"):
        w_in = a[name + "_w_in"]
        w[name + "_g"] = bf(w_in[:, :, :D_FF])
        w[name + "_u"] = bf(w_in[:, :, D_FF:])
        w[name + "_d"] = bf(a[name + "_w_out"])
    w["norm_ffn1"], w["norm_ffn2"], w["norm_mix"] = vec(a["norm_ffn1"]), vec(a["norm_ffn2"]), vec(a["norm_mix"])

    splits = (D_CONV, D_CONV, D_CONV, D_LRU, D_LRU, Q_LORA, KV_LORA, ROPE_DIM, MEM_WIDTH) + (D,) * N_BRANCH
    off = np.concatenate([[0], np.cumsum(splits)])
    w_in = a["w_in"]
    piece = lambda i: w_in[:, :, off[i]:off[i + 1]]
    zeros = lambda n: jnp.zeros((depth, D, n), w_in.dtype)
    half = ROPE_DIM // 2
    c_kr = piece(7)
    kr_a = jnp.concatenate([c_kr, zeros(LANES - ROPE_DIM)], axis=-1)
    kr_b = jnp.concatenate([-c_kr[..., half:], c_kr[..., :half], zeros(LANES - ROPE_DIM)], axis=-1)
    w["w_in"] = bf(jnp.concatenate(
        [w_in[:, :, off[9]:]] + [piece(i) for i in range(7)] + [kr_a, kr_b, piece(8), zeros(LANES)], axis=-1))
    assert w["w_in"].shape[-1] == P_WIDTH

    def slot_gain(g):
        return jnp.concatenate([g[:, NOPE_DIM:], g[:, :NOPE_DIM],
                                jnp.zeros((depth, HEAD_SLOT - QK_DIM), g.dtype)], axis=-1)[:, None, :]

    uq = a["mla_w_uq"].reshape(depth, Q_LORA, MLA_HEADS, QK_DIM)
    nope, pe = uq[..., :NOPE_DIM], uq[..., NOPE_DIM:]
    z = lambda n: jnp.zeros((depth, Q_LORA, MLA_HEADS, n), uq.dtype)
    w["uq_a"] = bf(jnp.concatenate([pe, nope, z(HEAD_SLOT - QK_DIM)], axis=-1).reshape(depth, Q_LORA, -1))
    w["uq_b"] = bf(jnp.concatenate([-pe[..., half:], pe[..., :half], z(HEAD_SLOT - ROPE_DIM)],
                                   axis=-1).reshape(depth, Q_LORA, -1))
    w["q_norm"], w["kv_norm"] = vec(a["mla_q_norm"]), vec(a["mla_kv_norm"])
    w["q_gain"] = slot_gain(a["mla_q_gain"])
    w["k_gain"] = slot_gain(a["mla_k_gain"])
    w["k_gain_tiled"] = jnp.tile(w["k_gain"], (1, 1, MLA_HEADS))
    w["mq_gain"] = vec(a["mem_q_gain"])

    ukv = a["mla_w_ukv"].reshape(depth, KV_LORA, MLA_HEADS, NOPE_DIM + V_DIM)
    k_nope, vv = ukv[..., :NOPE_DIM], ukv[..., NOPE_DIM:]
    zk = lambda n: jnp.zeros((depth, KV_LORA, MLA_HEADS, n), ukv.dtype)
    wk_slot = jnp.concatenate([zk(ROPE_DIM), k_nope, zk(HEAD_SLOT - QK_DIM)], axis=-1)
    w["wk_slot"] = bf(wk_slot.reshape(depth, KV_LORA, -1))
    w["wk_t"] = bf(wk_slot.transpose(0, 2, 3, 1))
    w["wk_all"] = bf(k_nope.reshape(depth, KV_LORA, -1))
    w["wv"] = bf(vv.reshape(depth, KV_LORA, -1))

    gw = a["lru_gate_w"]
    blk = D_LRU // LRU_BLOCKS
    eye = jnp.eye(LRU_BLOCKS, dtype=gw.dtype)[None, :, None, :, None]
    dense = lambda part: (eye * part[:, :, :, None, :]).reshape(depth, D_LRU, D_LRU)
    w["lru_wr"], w["lru_wi"] = bf(dense(gw[..., :blk])), bf(dense(gw[..., blk:]))
    gb = a["lru_gate_b"]
    w["lru_br"] = gb[..., :blk].reshape(depth, 1, D_LRU)
    w["lru_bi"] = gb[..., blk:].reshape(depth, 1, D_LRU)
    w["lru_lambda"], w["conv_b_bias"] = vec(a["lru_lambda"]), vec(a["conv_b_bias"])
    w["conv_a_w"], w["conv_b_w"] = a["conv_a_w"], a["conv_b_w"]

    w["mem_norm"], w["mem_k_gain"] = vec(a["mem_norm"]), vec(a["mem_k_gain"])
    w["mem_w_kv"] = bf(a["mem_w_kv"])
    w["w_branch"], w["w_out"] = bf(a["w_branch"]), bf(a["w_out"])
    return w


def _rope_tables(pos):
    half = ROPE_DIM // 2
    inv = ROPE_THETA ** (-jnp.arange(half, dtype=F32) / half)
    ang = pos.astype(F32)[:, None] * inv[None, :]
    cos, sin = jnp.cos(ang), jnp.sin(ang)
    n = pos.shape[0]
    cos_t = jnp.concatenate([cos, cos, jnp.ones((n, NOPE_DIM), F32), jnp.zeros((n, HEAD_SLOT - QK_DIM), F32)], 1)
    sin_t = jnp.concatenate([sin, sin, jnp.zeros((n, HEAD_SLOT - ROPE_DIM), F32)], axis=1)
    return cos_t, sin_t


def _tile_sizes(T, S, Ts):
    pick = lambda n, opts: next(o for o in opts if n % o == 0)
    return dict(
        tm=pick(np.gcd(T, S), (512, 256, 128)),
        tt=pick(S, (512, 256, 128)),
        ts=pick(Ts, (256, 128, 64, 32, 16, 8)),
        tq=pick(S, (512, 256, 128)),
    )


def kernel(x_prompt, x_sample, cache_kv_latent, cache_k_rope, cache_mem_k, cache_mem_v, state_conv_a, state_conv_b, state_lru_h, page_table, mem_prompt, norm_ffn1, ffn1_w_in, ffn1_w_out, norm_mix, w_in, conv_a_w, conv_b_w, conv_b_bias, lru_gate_w, lru_gate_b, lru_lambda, mla_q_norm, mla_w_uq, mla_kv_norm, mla_w_ukv, mla_q_gain, mla_k_gain, mem_norm, mem_w_kv, mem_k_gain, mem_q_gain, w_branch, w_out, norm_ffn2, ffn2_w_in, ffn2_w_out):
    B, S, D = x_prompt.shape
    DB, DS, _ = x_sample.shape
    depth = norm_ffn1.shape[0]
    page = cache_kv_latent.shape[2]
    past = page_table.shape[1] * page
    Tp, Ts = B * S, DB * DS
    T = Tp + Ts
    ts_ = _tile_sizes(T, S, Ts)
    tm, tt, ts, tq = ts_["tm"], ts_["tt"], ts_["ts"], ts_["tq"]
    assert Tp % ts == 0 and Tp % DS == 0

    w = _prepare_weights(dict(
        norm_ffn1=norm_ffn1, ffn1_w_in=ffn1_w_in, ffn1_w_out=ffn1_w_out, norm_mix=norm_mix, w_in=w_in,
        conv_a_w=conv_a_w, conv_b_w=conv_b_w, conv_b_bias=conv_b_bias, lru_gate_w=lru_gate_w,
        lru_gate_b=lru_gate_b, lru_lambda=lru_lambda, mla_q_norm=mla_q_norm, mla_w_uq=mla_w_uq,
        mla_kv_norm=mla_kv_norm, mla_w_ukv=mla_w_ukv, mla_q_gain=mla_q_gain, mla_k_gain=mla_k_gain,
        mem_norm=mem_norm, mem_w_kv=mem_w_kv, mem_k_gain=mem_k_gain, mem_q_gain=mem_q_gain,
        w_branch=w_branch, w_out=w_out, norm_ffn2=norm_ffn2, ffn2_w_in=ffn2_w_in, ffn2_w_out=ffn2_w_out))

    pos = jnp.concatenate([jnp.tile(jnp.arange(S), B), jnp.tile(past + jnp.arange(DS), DB)])
    cos_t, sin_t = _rope_tables(pos)

    ka, kb = state_conv_a.shape[2], state_conv_b.shape[2]
    sa = jnp.pad(state_conv_a, ((0, 0), (0, 0), (DS - ka, 0), (0, 0))).reshape(depth, Ts, D_CONV)
    sb = jnp.pad(state_conv_b, ((0, 0), (0, 0), (DS - kb, 0), (0, 0))).reshape(depth, Ts, D_LRU)
    h0 = jnp.pad(state_lru_h[:, :, None, :], ((0, 0), (0, 0), (0, DS - 1), (0, 0))).reshape(depth, Ts, D_LRU)
    cmk = cache_mem_k.reshape(depth, DB, N_MEM, MEM_WIDTH)
    cmv = cache_mem_v.reshape(depth, DB, N_MEM, MEM_WIDTH)

    mk_p, mv_p = _memkv(mem_prompt.reshape(B * N_MEM, D), w["mem_norm"], w["mem_w_kv"], w["mem_k_gain"])
    mk_p4 = mk_p.reshape(depth, B, N_MEM, MEM_WIDTH)
    mv_p4 = mv_p.reshape(depth, B, N_MEM, MEM_WIDTH)

    x = jnp.concatenate([x_prompt.reshape(Tp, D), x_sample.reshape(Ts, D)], axis=0)
    lats, krs, p_ca, p_cb, p_h, s_ca, s_cb, s_h = [], [], [], [], [], [], [], []
    n_pages = page_table.shape[1]
    G = next(g for g in (8, 4, 2, 1) if n_pages % g == 0)
    nb = next(n for n in (8, 4, 2, 1) if DB % n == 0)
    for l in range(depth):
        x = _ffn(x, w["norm_ffn1"], w["ffn1_g"], w["ffn1_u"], w["ffn1_d"], l, tm)
        P = _proj(x, w["norm_mix"], w["w_in"], l, tm)
        q, lat, kr, k, v, qm = _prep(P, cos_t, sin_t, w, l, tm)
        ya_p, yr_p, ta, tb, th = _convlru_prompt(P, w, l, B, S, tt)
        ya_s, yr_s, va_s, h_s = _convlru_sample(P, sa, sb, h0, w, l, Tp, Ts, DS, ts)
        yc_p = _flash(q, k, v, B, S, tq, tq)
        yc_s = _sattn(page_table, q[Tp:].astype(F32), lat, kr, cache_kv_latent, cache_k_rope, w, l,
                      Tp, DB, DS, G)
        ym_p = _mem_prompt(qm, mk_p4, mv_p4, l, B, S, tm)
        ym_s = _mem_sample(qm[Tp:], cmk, cmv, l, DB, DS, nb)
        ya = jnp.concatenate([ya_p, ya_s], axis=0)
        yr = jnp.concatenate([yr_p, yr_s], axis=0)
        yc = jnp.concatenate([yc_p, yc_s.astype(BF16)], axis=0)
        ym = jnp.concatenate([ym_p, ym_s.astype(BF16)], axis=0)
        x = _merge(x, P, ya, yr, yc, ym, w["w_branch"], w["w_out"], l, tm)
        x = _ffn(x, w["norm_ffn2"], w["ffn2_g"], w["ffn2_u"], w["ffn2_d"], l, tm)

        lats.append(lat)
        krs.append(kr[:, :ROPE_DIM])
        p_ca.append(ta[:, SUBLANES - ka:])
        p_cb.append(tb[:, SUBLANES - kb:])
        p_h.append(th[:, SUBLANES - 1])
        s_ca.append(va_s.reshape(DB, DS, D_CONV)[:, DS - ka:])
        s_cb.append(P[Tp:, P_CONV + 3 * D_CONV:P_CONV + 3 * D_CONV + D_LRU].reshape(DB, DS, D_LRU)[:, DS - kb:])
        s_h.append(h_s.reshape(DB, DS, D_LRU)[:, DS - 1])

    lat_all = jnp.stack(lats)
    kr_all = jnp.stack(krs)
    return (
        x[:Tp].reshape(B, S, D), x[Tp:].reshape(DB, DS, D),
        lat_all[:, :Tp].reshape(depth, B, S, KV_LORA), kr_all[:, :Tp].reshape(depth, B, S, ROPE_DIM),
        jnp.stack(p_ca), jnp.stack(p_cb), jnp.stack(p_h),
        mk_p.reshape(depth, B, N_MEM, MEM_HEADS, MEM_HD), mv_p.reshape(depth, B, N_MEM, MEM_HEADS, MEM_HD),
        lat_all[:, Tp:].reshape(depth, DB, DS, KV_LORA), kr_all[:, Tp:].reshape(depth, DB, DS, ROPE_DIM),
        jnp.stack(s_ca), jnp.stack(s_cb), jnp.stack(s_h),
    )
```

```python
import functools

import numpy as np
import jax
import jax.numpy as jnp
from jax import lax
from jax.experimental import pallas as pl
from jax.experimental.pallas import tpu as pltpu

F32 = jnp.float32
BF16 = jnp.bfloat16

D_FF = 2816
D_CONV = 512
D_LRU = 512
LRU_BLOCKS = 8
LRU_C = 8.0
MLA_HEADS = 8
Q_LORA = 384
KV_LORA = 256
NOPE_DIM = 64
ROPE_DIM = 32
V_DIM = 64
QK_DIM = NOPE_DIM + ROPE_DIM
ROPE_THETA = 10000.0
N_MEM = 256
MEM_HEADS = 4
MEM_HD = 128
MEM_WIDTH = MEM_HEADS * MEM_HD
N_BRANCH = 4
EPS = 1e-6
NEG = -1e30

LANES = 128
SUBLANES = 8
HEAD_SLOT = LANES
VMEM_LIMIT = 56 * 1024 * 1024
LOG2E = 1.4426950408889634
NK = MLA_HEADS * NOPE_DIM

P_GATES = 0
P_CONV = 4096
P_TAIL = 6144
P_WIDTH = 8192
P_CHUNK = 2048
T_RY, T_CQ, T_CKV, T_KRA, T_KRB, T_MQ = 0, 512, 896, 1152, 1280, 1408


def _dot(a, b):
    return jnp.dot(a, b, preferred_element_type=F32)


def _dot_nt(a, b):
    return lax.dot_general(a, b, (((1,), (1,)), ((), ())), preferred_element_type=F32)


def _rms(x, g):
    return x * lax.rsqrt(jnp.mean(x * x, axis=-1, keepdims=True) + EPS) * g


def _params(*sem):
    return pltpu.CompilerParams(dimension_semantics=sem, vmem_limit_bytes=VMEM_LIMIT)


def _ffn_kernel(x_ref, g_ref, wg_ref, wu_ref, wd_ref, o_ref, xn_ref, acc_ref, *, nf):
    j = pl.program_id(1)

    @pl.when(j == 0)
    def _():
        xn_ref[...] = _rms(x_ref[...], g_ref[...]).astype(BF16)
        acc_ref[...] = jnp.zeros_like(acc_ref)

    xn = xn_ref[...]
    g = _dot(xn, wg_ref[...])
    u = _dot(xn, wu_ref[...])
    h = (g * jax.nn.sigmoid(g) * u).astype(BF16)
    acc_ref[...] += _dot(h, wd_ref[...])

    @pl.when(j == nf - 1)
    def _():
        o_ref[...] = x_ref[...] + 0.5 * acc_ref[...]


def _ffn(x, gain, wg, wu, wd, l, tm):
    T, D = x.shape
    nf = 2
    tf = wg.shape[2] // nf
    return pl.pallas_call(
        functools.partial(_ffn_kernel, nf=nf),
        grid=(T // tm, nf),
        in_specs=[
            pl.BlockSpec((tm, D), lambda i, j: (i, 0)),
            pl.BlockSpec((None, 1, D), lambda i, j: (l, 0, 0)),
            pl.BlockSpec((None, D, tf), lambda i, j: (l, 0, j)),
            pl.BlockSpec((None, D, tf), lambda i, j: (l, 0, j)),
            pl.BlockSpec((None, tf, D), lambda i, j: (l, j, 0)),
        ],
        out_specs=pl.BlockSpec((tm, D), lambda i, j: (i, 0)),
        out_shape=jax.ShapeDtypeStruct((T, D), F32),
        scratch_shapes=[pltpu.VMEM((tm, D), BF16), pltpu.VMEM((tm, D), F32)],
        compiler_params=_params("parallel", "arbitrary"),
        name="ffn",
    )(x, gain, wg, wu, wd)


def _proj_kernel(x_ref, g_ref, w_ref, o_ref, xn_ref, *, n_gate_chunks):
    j = pl.program_id(1)

    @pl.when(j == 0)
    def _():
        xn_ref[...] = _rms(x_ref[...], g_ref[...]).astype(BF16)

    y = _dot(xn_ref[...], w_ref[...])

    @pl.when(j < n_gate_chunks)
    def _():
        o_ref[...] = jax.nn.sigmoid(y).astype(o_ref.dtype)

    @pl.when(j >= n_gate_chunks)
    def _():
        o_ref[...] = y.astype(o_ref.dtype)


def _proj(x, gain, w, l, tm):
    T, D = x.shape
    return pl.pallas_call(
        functools.partial(_proj_kernel, n_gate_chunks=P_CONV // P_CHUNK),
        grid=(T // tm, P_WIDTH // P_CHUNK),
        in_specs=[
            pl.BlockSpec((tm, D), lambda i, j: (i, 0)),
            pl.BlockSpec((None, 1, D), lambda i, j: (l, 0, 0)),
            pl.BlockSpec((None, D, P_CHUNK), lambda i, j: (l, 0, j)),
        ],
        out_specs=pl.BlockSpec((tm, P_CHUNK), lambda i, j: (i, j)),
        out_shape=jax.ShapeDtypeStruct((T, P_WIDTH), BF16),
        scratch_shapes=[pltpu.VMEM((tm, D), BF16)],
        compiler_params=_params("parallel", "arbitrary"),
        name="proj",
    )(x, gain, w)


def _head_norm_store(dst_ref, x, gain, n_heads, inv_width):
    for h in range(n_heads):
        sl = slice(h * HEAD_SLOT, (h + 1) * HEAD_SLOT)
        xh = x[:, sl]
        ms = jnp.sum(xh * xh, axis=-1, keepdims=True) * inv_width
        dst_ref[:, sl] = (xh * lax.rsqrt(ms + EPS) * gain).astype(dst_ref.dtype)


def _prep_kernel(p_ref, cos_ref, sin_ref, qn_ref, wa_ref, wb_ref, qg_ref, kvn_ref, wk_ref, wv_ref,
                 kg_ref, mqg_ref, q_ref, lat_ref, kr_ref, k_ref, v_ref, qm_ref):
    cos = cos_ref[...]
    sin = sin_ref[...]
    n = _rms(p_ref[:, T_CQ:T_CQ + Q_LORA].astype(F32), qn_ref[...]).astype(BF16)
    qa = _dot(n, wa_ref[...])
    qb = _dot(n, wb_ref[...])
    qg = qg_ref[...] * (QK_DIM ** -0.5 * LOG2E)
    for h in range(MLA_HEADS):
        sl = slice(h * HEAD_SLOT, (h + 1) * HEAD_SLOT)
        qh = qa[:, sl] * cos + qb[:, sl] * sin
        ms = jnp.sum(qh * qh, axis=-1, keepdims=True) * (1.0 / QK_DIM)
        q_ref[:, sl] = (qh * lax.rsqrt(ms + EPS) * qg).astype(q_ref.dtype)
    lat = _rms(p_ref[:, T_CKV:T_CKV + KV_LORA].astype(F32), kvn_ref[...])
    lat_ref[...] = lat
    kr = p_ref[:, T_KRA:T_KRA + LANES].astype(F32) * cos + p_ref[:, T_KRB:T_KRB + LANES].astype(F32) * sin
    kr_ref[...] = kr
    latb = lat.astype(BF16)
    kn = _dot(latb, wk_ref[...])
    kg = kg_ref[...]
    for h in range(MLA_HEADS):
        sl = slice(h * HEAD_SLOT, (h + 1) * HEAD_SLOT)
        kh = kn[:, sl] + kr
        ms = jnp.sum(kh * kh, axis=-1, keepdims=True) * (1.0 / QK_DIM)
        k_ref[:, sl] = (kh * lax.rsqrt(ms + EPS) * kg).astype(k_ref.dtype)
    v_ref[...] = _dot(latb, wv_ref[...]).astype(v_ref.dtype)
    _head_norm_store(qm_ref, p_ref[:, T_MQ:T_MQ + MEM_WIDTH].astype(F32), mqg_ref[...] * (MEM_HD ** -0.5),
                     MEM_HEADS, 1.0 / MEM_HD)


def _prep(P, cos, sin, w, l, tm):
    T = P.shape[0]
    HW = MLA_HEADS * HEAD_SLOT
    lay = lambda *shape: pl.BlockSpec((None,) + shape, lambda i: (l,) + (0,) * len(shape))
    row = lambda width: pl.BlockSpec((tm, width), lambda i: (i, 0))
    return pl.pallas_call(
        _prep_kernel,
        grid=(T // tm,),
        in_specs=[
            pl.BlockSpec((tm, P_CHUNK), lambda i: (i, P_TAIL // P_CHUNK)),
            row(LANES), row(LANES),
            lay(1, Q_LORA), lay(Q_LORA, HW), lay(Q_LORA, HW), lay(1, HEAD_SLOT),
            lay(1, KV_LORA), lay(KV_LORA, HW), lay(KV_LORA, MLA_HEADS * V_DIM), lay(1, HEAD_SLOT),
            lay(1, MEM_HD),
        ],
        out_specs=[row(HW), row(KV_LORA), row(LANES), row(HW), row(MLA_HEADS * V_DIM), row(MEM_WIDTH)],
        out_shape=[
            jax.ShapeDtypeStruct((T, HW), BF16),
            jax.ShapeDtypeStruct((T, KV_LORA), F32),
            jax.ShapeDtypeStruct((T, LANES), F32),
            jax.ShapeDtypeStruct((T, HW), BF16),
            jax.ShapeDtypeStruct((T, MLA_HEADS * V_DIM), BF16),
            jax.ShapeDtypeStruct((T, MEM_WIDTH), BF16),
        ],
        compiler_params=_params("parallel"),
        name="prep",
    )(P, cos, sin, w["q_norm"], w["uq_a"], w["uq_b"], w["q_gain"], w["kv_norm"], w["wk_slot"],
      w["wv"], w["k_gain"], w["mq_gain"])


def _lru_coeffs(xr, wr, wi, br, bi, lam):
    xb = xr.astype(BF16)
    r = jax.nn.sigmoid(_dot(xb, wr) + br)
    i = jax.nn.sigmoid(_dot(xb, wi) + bi)
    log_a = -LRU_C * r * jax.nn.softplus(-lam)
    a = jnp.exp(log_a)
    b = jnp.sqrt(1.0 - a * a) * (i * xr)
    return a, b


def _scan(a, b, pos, seg):
    d = 1
    while d < seg:
        keep = pos >= d
        a_s = jnp.where(keep, pltpu.roll(a, d, 0), 1.0)
        b_s = jnp.where(keep, pltpu.roll(b, d, 0), 0.0)
        b = a * b_s + b
        a = a * a_s
        d *= 2
    return a, b


def _conv_taps(x, shifted, w, bias=None):
    K = w.shape[0]
    y = shifted(K - 1) * w[0:1]
    for k in range(1, K - 1):
        y = y + shifted(K - 1 - k) * w[k:k + 1]
    y = y + x * w[K - 1:K]
    if bias is not None:
        y = y + bias
    return y


def _convlru_prompt_kernel(p_ref, ry_ref, caw_ref, cbw_ref, cbb_ref, wr_ref, wi_ref, br_ref, bi_ref,
                           lam_ref, ya_ref, yr_ref, ta_ref, tb_ref, th_ref, pa_ref, pb_ref, h_ref):
    tt = p_ref.shape[0]

    @pl.when(pl.program_id(1) == 0)
    def _():
        pa_ref[...] = jnp.zeros_like(pa_ref)
        pb_ref[...] = jnp.zeros_like(pb_ref)
        h_ref[...] = jnp.zeros_like(h_ref)

    row = lax.broadcasted_iota(jnp.int32, (tt, 1), 0)
    row8 = lax.broadcasted_iota(jnp.int32, (SUBLANES, 1), 0)

    def delayed(x, prev8):
        def shifted(s):
            r = pltpu.roll(x, s, 0)
            first = jnp.where(row8 < s, pltpu.roll(prev8, s, 0), r[:SUBLANES])
            return jnp.concatenate([first, r[SUBLANES:]], axis=0)
        return shifted

    a_b = p_ref[:, 0:D_CONV].astype(F32)
    v = p_ref[:, D_CONV:2 * D_CONV].astype(F32) * p_ref[:, 2 * D_CONV:3 * D_CONV].astype(F32)
    u = _conv_taps(v, delayed(v, pa_ref[...]), caw_ref[...])
    ya_ref[...] = (a_b * u).astype(ya_ref.dtype)

    rx = p_ref[:, 3 * D_CONV:3 * D_CONV + D_LRU].astype(F32)
    xr = _conv_taps(rx, delayed(rx, pb_ref[...]), cbw_ref[...], cbb_ref[...])
    a, b = _lru_coeffs(xr, wr_ref[...], wi_ref[...], br_ref[...], bi_ref[...], lam_ref[...])
    a, b = _scan(a, b, row, tt)
    h = b + a * h_ref[...]
    yr_ref[...] = (h * jax.nn.gelu(ry_ref[...].astype(F32))).astype(yr_ref.dtype)

    tail = slice(tt - SUBLANES, tt)
    ta_ref[...] = v[tail]
    tb_ref[...] = rx[tail]
    th_ref[...] = h[tail]
    pa_ref[...] = v[tail]
    pb_ref[...] = rx[tail]
    h_ref[...] = h[tt - 1:tt]


def _convlru_prompt(P, w, l, B, S, tt):
    nt = S // tt
    cb = P_CHUNK // D_CONV
    lay = lambda *shape: pl.BlockSpec((None,) + shape, lambda b, t: (l,) + (0,) * len(shape))
    seq = pl.BlockSpec((tt, D_CONV), lambda b, t: (b * nt + t, 0))
    tail = pl.BlockSpec((None, SUBLANES, D_CONV), lambda b, t: (b, 0, 0))
    tail_shape = jax.ShapeDtypeStruct((B, SUBLANES, D_CONV), F32)
    return pl.pallas_call(
        _convlru_prompt_kernel,
        grid=(B, nt),
        in_specs=[
            pl.BlockSpec((tt, P_CHUNK), lambda b, t: (b * nt + t, P_CONV // P_CHUNK)),
            pl.BlockSpec((tt, D_LRU), lambda b, t: (b * nt + t, P_TAIL // D_LRU)),
            lay(3, D_CONV), lay(4, D_LRU), lay(1, D_LRU), lay(D_LRU, D_LRU), lay(D_LRU, D_LRU),
            lay(1, D_LRU), lay(1, D_LRU), lay(1, D_LRU),
        ],
        out_specs=[seq, seq, tail, tail, tail],
        out_shape=[
            jax.ShapeDtypeStruct((B * S, D_CONV), BF16),
            jax.ShapeDtypeStruct((B * S, D_LRU), BF16),
            tail_shape, tail_shape, tail_shape,
        ],
        scratch_shapes=[pltpu.VMEM((SUBLANES, D_CONV), F32), pltpu.VMEM((SUBLANES, D_LRU), F32),
                        pltpu.VMEM((1, D_LRU), F32)],
        compiler_params=_params("parallel", "arbitrary"),
        name="convlru_prompt",
    )(P, P, w["conv_a_w"], w["conv_b_w"], w["conv_b_bias"], w["lru_wr"], w["lru_wi"], w["lru_br"],
      w["lru_bi"], w["lru_lambda"])


def _convlru_sample_kernel(p_ref, ry_ref, sa_ref, sb_ref, h0_ref, caw_ref, cbw_ref, cbb_ref, wr_ref,
                           wi_ref, br_ref, bi_ref, lam_ref, ya_ref, yr_ref, va_ref, h_ref, *, ds):
    ts = p_ref.shape[0]
    pos = lax.broadcasted_iota(jnp.int32, (ts, 1), 0) % ds

    def delayed(x, state):
        def shifted(s):
            return jnp.where(pos < s, pltpu.roll(state, ts - ds + s, 0), pltpu.roll(x, s, 0))
        return shifted

    a_b = p_ref[:, 0:D_CONV].astype(F32)
    v = p_ref[:, D_CONV:2 * D_CONV].astype(F32) * p_ref[:, 2 * D_CONV:3 * D_CONV].astype(F32)
    u = _conv_taps(v, delayed(v, sa_ref[...]), caw_ref[...])
    ya_ref[...] = (a_b * u).astype(ya_ref.dtype)
    va_ref[...] = v

    rx = p_ref[:, 3 * D_CONV:3 * D_CONV + D_LRU].astype(F32)
    xr = _conv_taps(rx, delayed(rx, sb_ref[...]), cbw_ref[...], cbb_ref[...])
    a, b = _lru_coeffs(xr, wr_ref[...], wi_ref[...], br_ref[...], bi_ref[...], lam_ref[...])
    b = b + a * h0_ref[...]
    _, h = _scan(a, b, pos, ds)
    h_ref[...] = h
    yr_ref[...] = (h * jax.nn.gelu(ry_ref[...].astype(F32))).astype(yr_ref.dtype)


def _convlru_sample(P, sa, sb, h0, w, l, row0, Ts, ds, ts):
    assert ds == SUBLANES
    r0 = row0 // ts
    lay = lambda *shape: pl.BlockSpec((None,) + shape, lambda i: (l,) + (0,) * len(shape))
    st = pl.BlockSpec((None, ts, D_CONV), lambda i: (l, i, 0))
    seq = pl.BlockSpec((ts, D_CONV), lambda i: (i, 0))
    return pl.pallas_call(
        functools.partial(_convlru_sample_kernel, ds=ds),
        grid=(Ts // ts,),
        in_specs=[
            pl.BlockSpec((ts, P_CHUNK), lambda i: (r0 + i, P_CONV // P_CHUNK)),
            pl.BlockSpec((ts, D_LRU), lambda i: (r0 + i, P_TAIL // D_LRU)),
            st, st, st,
            lay(3, D_CONV), lay(4, D_LRU), lay(1, D_LRU), lay(D_LRU, D_LRU), lay(D_LRU, D_LRU),
            lay(1, D_LRU), lay(1, D_LRU), lay(1, D_LRU),
        ],
        out_specs=[seq, seq, seq, seq],
        out_shape=[
            jax.ShapeDtypeStruct((Ts, D_CONV), BF16),
            jax.ShapeDtypeStruct((Ts, D_LRU), BF16),
            jax.ShapeDtypeStruct((Ts, D_CONV), F32),
            jax.ShapeDtypeStruct((Ts, D_LRU), F32),
        ],
        compiler_params=_params("parallel"),
        name="convlru_sample",
    )(P, P, sa, sb, h0, w["conv_a_w"], w["conv_b_w"], w["conv_b_bias"], w["lru_wr"], w["lru_wi"],
      w["lru_br"], w["lru_bi"], w["lru_lambda"])


def _flash_kernel(q_ref, k_ref, v_ref, o_ref, *, tq, ch):
    S = k_ref.shape[0]
    qi = pl.program_id(2)
    n_chunks = (qi * tq + tq + ch - 1) // ch
    first_head = lax.broadcasted_iota(jnp.int32, (1, LANES), 1) < V_DIM
    qpos = qi * tq + lax.broadcasted_iota(jnp.int32, (tq, 1), 0)

    def body(n):
        L = n * ch
        v = v_ref[0:L, :]
        one = jnp.ones_like(v)
        vext = (jnp.where(first_head, v, one), jnp.where(first_head, one, v))
        mask = lax.broadcasted_iota(jnp.int32, (1, L), 1) <= qpos
        outs = []
        for hh in range(2):
            sl = slice(hh * HEAD_SLOT, (hh + 1) * HEAD_SLOT)
            s = jnp.where(mask, _dot_nt(q_ref[:, sl], k_ref[0:L, sl]), NEG)
            m = jnp.max(s, axis=-1, keepdims=True)
            a = _dot(jnp.exp2(s - m).astype(BF16), vext[hh])
            outs.append(a / pltpu.roll(a, V_DIM, 1))
        o_ref[...] = jnp.where(first_head, outs[0], outs[1]).astype(o_ref.dtype)

    for n in range(1, S // ch + 1):
        pl.when(n_chunks == n)(functools.partial(body, n))


def _flash(q, k, v, B, S, tq, ch):
    nq = S // tq
    hp = MLA_HEADS // 2
    return pl.pallas_call(
        functools.partial(_flash_kernel, tq=tq, ch=ch),
        grid=(B, hp, nq),
        in_specs=[
            pl.BlockSpec((tq, 2 * HEAD_SLOT), lambda b, h, i: (b * nq + i, h)),
            pl.BlockSpec((S, 2 * HEAD_SLOT), lambda b, h, i: (b, h)),
            pl.BlockSpec((S, 2 * V_DIM), lambda b, h, i: (b, h)),
        ],
        out_specs=pl.BlockSpec((tq, 2 * V_DIM), lambda b, h, i: (b * nq + i, h)),
        out_shape=jax.ShapeDtypeStruct((B * S, MLA_HEADS * V_DIM), BF16),
        compiler_params=_params("parallel", "parallel", "arbitrary"),
        name="flash",
    )(q, k, v)


def _sattn_kernel(pt_ref, q_ref, latn_ref, krn_ref, wkn_ref, wkt_ref, wv_ref, kg_ref, clat_ref, ckrt_ref,
                  o_ref, a_ref, lat_st, kr_st, sem, *, l, n_pages, page, ds, nsub):
    b = pl.program_id(0)
    nb = pl.num_programs(0)
    R = MLA_HEADS * ds
    n = n_pages * page
    sub = n // nsub

    def page_copies(seq, slot, i):
        pid = pt_ref[seq, i]
        dst = pl.ds(pl.multiple_of(i * page, page), page)
        return (pltpu.make_async_copy(clat_ref.at[l, pid], lat_st.at[slot, dst, :], sem.at[0, slot]),
                pltpu.make_async_copy(ckrt_ref.at[l, pid], kr_st.at[slot, :, dst], sem.at[1, slot]))

    def start_all(seq, slot):
        def body(i, c):
            for cp in page_copies(seq, slot, i):
                cp.start()
            return c
        lax.fori_loop(0, n_pages, body, 0)

    def wait_all(seq, slot):
        def body(i, c):
            for cp in page_copies(seq, slot, i):
                cp.wait()
            return c
        lax.fori_loop(0, n_pages, body, 0)

    slot = b % 2

    @pl.when(b == 0)
    def _():
        start_all(0, 0)

    @pl.when(b + 1 < nb)
    def _():
        start_all(b + 1, 1 - slot)

    qk = q_ref[...] * kg_ref[...]
    qa, qp = [], []
    for h in range(MLA_HEADS):
        qh = qk[:, h * HEAD_SLOT:(h + 1) * HEAD_SLOT]
        qa.append(_dot(qh.astype(BF16), wkt_ref[h]))
        qp.append(qh[:, 0:ROPE_DIM])
    a_ref[0:NK, :] = wkn_ref[...]
    a_ref[NK:NK + R, :] = jnp.concatenate(qa, axis=0).astype(BF16)
    qpe = jnp.concatenate(qp, axis=0).astype(BF16)
    ones_pe = jnp.ones((MLA_HEADS, ROPE_DIM), BF16)

    def update(carry, L, s_pe, ssq_pe, mask):
        m_prev, l_prev, acc = carry
        u = _dot_nt(a_ref[...], L)
        kn = u[0:NK].reshape(MLA_HEADS, NOPE_DIM, -1)
        ssq = jnp.sum(kn * kn, axis=1) + ssq_pe
        rinv = lax.rsqrt(ssq * (1.0 / QK_DIM) + EPS)
        s = ((u[NK:NK + R] + s_pe).reshape(MLA_HEADS, ds, -1) * rinv[:, None, :]).reshape(R, -1)
        if mask is not None:
            s = jnp.where(mask, s, NEG)
        m_new = jnp.maximum(m_prev, jnp.max(s, axis=-1, keepdims=True))
        al = jnp.exp2(m_prev - m_new)
        pr = jnp.exp2(s - m_new)
        l_new = al * l_prev + jnp.sum(pr, axis=-1, keepdims=True)
        return m_new, l_new, al * acc + _dot(pr.astype(BF16), L)

    Ln = jnp.concatenate([latn_ref[...], jnp.zeros((page - ds, KV_LORA), F32)], axis=0).astype(BF16)
    Rn = jnp.concatenate([krn_ref[:, 0:ROPE_DIM], jnp.zeros((page - ds, ROPE_DIM), F32)], axis=0)
    key = lax.broadcasted_iota(jnp.int32, (R, page), 1)
    qry = lax.broadcasted_iota(jnp.int32, (R, page), 0) % ds
    carry = (jnp.full((R, 1), NEG, F32), jnp.zeros((R, 1), F32), jnp.zeros((R, KV_LORA), F32))
    carry = update(carry, Ln, _dot_nt(qpe, Rn.astype(BF16)), _dot_nt(ones_pe, (Rn * Rn).astype(BF16)),
                   key <= qry)

    wait_all(b, slot)
    for c in range(nsub):
        L = lat_st[slot, c * sub:(c + 1) * sub, :].astype(BF16)
        rt = kr_st[slot, :, c * sub:(c + 1) * sub]
        rf = rt.astype(F32)
        carry = update(carry, L, _dot(qpe, rt), _dot(ones_pe, (rf * rf).astype(BF16)), None)

    _, l_fin, acc = carry
    o_lat = (acc / l_fin).astype(BF16)
    y = _dot(o_lat, wv_ref[...])
    vrow = lax.broadcasted_iota(jnp.int32, (R, MLA_HEADS * V_DIM), 0) // ds
    vcol = lax.broadcasted_iota(jnp.int32, (R, MLA_HEADS * V_DIM), 1) // V_DIM
    y = jnp.where(vrow == vcol, y, 0.0)
    o_ref[...] = jnp.sum(y.reshape(MLA_HEADS, ds, MLA_HEADS * V_DIM), axis=0)


def _sattn(page_table, q_s, lat, kr, cache_lat, cache_krt, w, l, row0, DB, ds, nsub):
    n_pages = page_table.shape[1]
    page = cache_lat.shape[2]
    assert ds == SUBLANES and MLA_HEADS == SUBLANES and (n_pages * page) % nsub == 0
    HW = MLA_HEADS * HEAD_SLOT
    R = MLA_HEADS * ds
    r0 = row0 // ds
    lay = lambda *shape: pl.BlockSpec((None,) + shape, lambda b, pt: (l,) + (0,) * len(shape))
    grid_spec = pltpu.PrefetchScalarGridSpec(
        num_scalar_prefetch=1,
        grid=(DB,),
        in_specs=[
            pl.BlockSpec((ds, HW), lambda b, pt: (b, 0)),
            pl.BlockSpec((ds, KV_LORA), lambda b, pt: (r0 + b, 0)),
            pl.BlockSpec((ds, LANES), lambda b, pt: (r0 + b, 0)),
            lay(NK, KV_LORA), lay(MLA_HEADS, HEAD_SLOT, KV_LORA), lay(KV_LORA, MLA_HEADS * V_DIM), lay(1, HW),
            pl.BlockSpec(memory_space=pl.ANY), pl.BlockSpec(memory_space=pl.ANY),
        ],
        out_specs=pl.BlockSpec((ds, MLA_HEADS * V_DIM), lambda b, pt: (b, 0)),
        scratch_shapes=[
            pltpu.VMEM((NK + R, KV_LORA), BF16),
            pltpu.VMEM((2, n_pages * page, KV_LORA), F32),
            pltpu.VMEM((2, ROPE_DIM, n_pages * page), BF16),
            pltpu.SemaphoreType.DMA((2, 2)),
        ],
    )
    return pl.pallas_call(
        functools.partial(_sattn_kernel, l=l, n_pages=n_pages, page=page, ds=ds, nsub=nsub),
        grid_spec=grid_spec,
        out_shape=jax.ShapeDtypeStruct((DB * ds, MLA_HEADS * V_DIM), F32),
        compiler_params=_params("arbitrary"),
        name="sattn",
    )(page_table, q_s, lat, kr, w["wk_nt"], w["wk_t"], w["wv"], w["k_gain_tiled"], cache_lat, cache_krt)


def _softmax_rows(s):
    m = jnp.max(s, axis=-1, keepdims=True)
    p = jnp.exp(s - m)
    return p / jnp.sum(p, axis=-1, keepdims=True)


def _mem_prompt_kernel(q_ref, k_ref, v_ref, o_ref):
    for h in range(MEM_HEADS):
        sl = slice(h * MEM_HD, (h + 1) * MEM_HD)
        p = _softmax_rows(_dot_nt(q_ref[:, sl], k_ref[:, sl].astype(BF16)))
        o_ref[:, sl] = _dot(p.astype(BF16), v_ref[:, sl].astype(BF16)).astype(o_ref.dtype)


def _mem_prompt(qm, mk, mv, l, B, S, tm):
    nt = S // tm
    kv = pl.BlockSpec((None, None, N_MEM, MEM_WIDTH), lambda b, t: (l, b, 0, 0))
    seq = pl.BlockSpec((tm, MEM_WIDTH), lambda b, t: (b * nt + t, 0))
    return pl.pallas_call(
        _mem_prompt_kernel,
        grid=(B, nt),
        in_specs=[seq, kv, kv],
        out_specs=seq,
        out_shape=jax.ShapeDtypeStruct((B * S, MEM_WIDTH), BF16),
        compiler_params=_params("parallel", "parallel"),
        name="mem_prompt",
    )(qm, mk, mv)


def _mem_sample_kernel(q_ref, k_ref, v_ref, o_ref, *, nb, ds):
    q = q_ref[...].reshape(nb, ds, MEM_WIDTH)
    outs = []
    for h in range(MEM_HEADS):
        sl = slice(h * MEM_HD, (h + 1) * MEM_HD)
        s = jnp.einsum('bqd,bkd->bqk', q[:, :, sl], k_ref[:, :, sl].astype(BF16),
                       preferred_element_type=F32)
        p = _softmax_rows(s)
        outs.append(jnp.einsum('bqk,bkd->bqd', p.astype(BF16), v_ref[:, :, sl].astype(BF16),
                               preferred_element_type=F32))
    o_ref[...] = jnp.concatenate(outs, axis=-1).reshape(nb * ds, MEM_WIDTH)


def _mem_sample(qm_s, ck, cv, l, DB, ds, nb):
    kv = pl.BlockSpec((None, nb, N_MEM, MEM_WIDTH), lambda i: (l, i, 0, 0))
    seq = pl.BlockSpec((nb * ds, MEM_WIDTH), lambda i: (i, 0))
    return pl.pallas_call(
        functools.partial(_mem_sample_kernel, nb=nb, ds=ds),
        grid=(DB // nb,),
        in_specs=[seq, kv, kv],
        out_specs=seq,
        out_shape=jax.ShapeDtypeStruct((DB * ds, MEM_WIDTH), F32),
        compiler_params=_params("parallel"),
        name="mem_sample",
    )(qm_s, ck, cv)


def _memkv_kernel(x_ref, n_ref, w_ref, kg_ref, k_ref, v_ref):
    kv = _dot(_rms(x_ref[...], n_ref[...]).astype(BF16), w_ref[...])
    _head_norm_store(k_ref, kv[:, :MEM_WIDTH], kg_ref[...], MEM_HEADS, 1.0 / MEM_HD)
    v_ref[...] = kv[:, MEM_WIDTH:]


def _memkv(mem, mem_norm, w_kv, k_gain):
    depth = w_kv.shape[0]
    R, D = mem.shape
    lay = lambda *shape: pl.BlockSpec((None,) + shape, lambda l: (l,) + (0,) * len(shape))
    out = jax.ShapeDtypeStruct((depth, R, MEM_WIDTH), F32)
    return pl.pallas_call(
        _memkv_kernel,
        grid=(depth,),
        in_specs=[pl.BlockSpec((R, D), lambda l: (0, 0)), lay(1, D), lay(D, 2 * MEM_WIDTH), lay(1, MEM_HD)],
        out_specs=[lay(R, MEM_WIDTH), lay(R, MEM_WIDTH)],
        out_shape=[out, out],
        compiler_params=_params("parallel"),
        name="memkv",
    )(mem, mem_norm, w_kv, k_gain)


def _merge_kernel(x_ref, sg_ref, ya_ref, yr_ref, yc_ref, ym_ref, wb_ref, wo_ref, o_ref):
    D = x_ref.shape[1]
    merged = None
    for bi, y_ref in enumerate((ya_ref, yr_ref, yc_ref, ym_ref)):
        width = y_ref.shape[1]
        t = sg_ref[:, bi * D:(bi + 1) * D].astype(F32) * _dot(y_ref[...], wb_ref[bi * width:(bi + 1) * width, :])
        merged = t if merged is None else merged + t
    o_ref[...] = x_ref[...] + _dot(merged.astype(BF16), wo_ref[...])


def _merge(x, P, ya, yr, yc, ym, wb, wo, l, tm):
    T, D = x.shape
    row = lambda width: pl.BlockSpec((tm, width), lambda i: (i, 0))
    lay = lambda *shape: pl.BlockSpec((None,) + shape, lambda i: (l,) + (0,) * len(shape))
    return pl.pallas_call(
        _merge_kernel,
        grid=(T // tm,),
        in_specs=[row(D), row(N_BRANCH * D), row(D_CONV), row(D_LRU), row(MLA_HEADS * V_DIM),
                  row(MEM_WIDTH), lay(wb.shape[1], D), lay(D, D)],
        out_specs=row(D),
        out_shape=jax.ShapeDtypeStruct((T, D), F32),
        compiler_params=_params("parallel"),
        name="merge",
    )(x, P, ya, yr, yc, ym, wb, wo)


def _prepare_weights(a):
    depth, D = a["norm_ffn1"].shape
    bf = lambda t: t.astype(BF16)
    vec = lambda t: t.reshape(depth, 1, -1)
    w = {}
    for name in ("ffn1", "ffn2"):
        w_in = a[name + "_w_in"]
        w[name + "_g"] = bf(w_in[:, :, :D_FF])
        w[name + "_u"] = bf(w_in[:, :, D_FF:])
        w[name + "_d"] = bf(a[name + "_w_out"])
    w["norm_ffn1"], w["norm_ffn2"], w["norm_mix"] = vec(a["norm_ffn1"]), vec(a["norm_ffn2"]), vec(a["norm_mix"])

    splits = (D_CONV, D_CONV, D_CONV, D_LRU, D_LRU, Q_LORA, KV_LORA, ROPE_DIM, MEM_WIDTH) + (D,) * N_BRANCH
    off = np.concatenate([[0], np.cumsum(splits)])
    w_in = a["w_in"]
    piece = lambda i: w_in[:, :, off[i]:off[i + 1]]
    zeros = lambda n: jnp.zeros((depth, D, n), w_in.dtype)
    half = ROPE_DIM // 2
    c_kr = piece(7)
    kr_a = jnp.concatenate([c_kr, zeros(LANES - ROPE_DIM)], axis=-1)
    kr_b = jnp.concatenate([-c_kr[..., half:], c_kr[..., :half], zeros(LANES - ROPE_DIM)], axis=-1)
    w["w_in"] = bf(jnp.concatenate(
        [w_in[:, :, off[9]:]] + [piece(i) for i in range(7)] + [kr_a, kr_b, piece(8), zeros(LANES)], axis=-1))
    assert w["w_in"].shape[-1] == P_WIDTH

    def slot_gain(g):
        return jnp.concatenate([g[:, NOPE_DIM:], g[:, :NOPE_DIM],
                                jnp.zeros((depth, HEAD_SLOT - QK_DIM), g.dtype)], axis=-1)[:, None, :]

    uq = a["mla_w_uq"].reshape(depth, Q_LORA, MLA_HEADS, QK_DIM)
    nope, pe = uq[..., :NOPE_DIM], uq[..., NOPE_DIM:]
    z = lambda n: jnp.zeros((depth, Q_LORA, MLA_HEADS, n), uq.dtype)
    w["uq_a"] = bf(jnp.concatenate([pe, nope, z(HEAD_SLOT - QK_DIM)], axis=-1).reshape(depth, Q_LORA, -1))
    w["uq_b"] = bf(jnp.concatenate([-pe[..., half:], pe[..., :half], z(HEAD_SLOT - ROPE_DIM)],
                                   axis=-1).reshape(depth, Q_LORA, -1))
    w["q_norm"], w["kv_norm"] = vec(a["mla_q_norm"]), vec(a["mla_kv_norm"])
    w["q_gain"] = slot_gain(a["mla_q_gain"])
    w["k_gain"] = slot_gain(a["mla_k_gain"])
    w["k_gain_tiled"] = jnp.tile(w["k_gain"], (1, 1, MLA_HEADS))
    w["mq_gain"] = vec(a["mem_q_gain"])

    ukv = a["mla_w_ukv"].reshape(depth, KV_LORA, MLA_HEADS, NOPE_DIM + V_DIM)
    k_nope, vv = ukv[..., :NOPE_DIM], ukv[..., NOPE_DIM:]
    zk = lambda n: jnp.zeros((depth, KV_LORA, MLA_HEADS, n), ukv.dtype)
    wk_slot = jnp.concatenate([zk(ROPE_DIM), k_nope, zk(HEAD_SLOT - QK_DIM)], axis=-1)
    w["wk_slot"] = bf(wk_slot.reshape(depth, KV_LORA, -1))
    w["wk_t"] = bf(wk_slot.transpose(0, 2, 3, 1))
    w["wk_nt"] = bf(k_nope.reshape(depth, KV_LORA, -1).transpose(0, 2, 1))
    w["wv"] = bf(vv.reshape(depth, KV_LORA, -1))

    gw = a["lru_gate_w"]
    blk = D_LRU // LRU_BLOCKS
    eye = jnp.eye(LRU_BLOCKS, dtype=gw.dtype)[None, :, None, :, None]
    dense = lambda part: (eye * part[:, :, :, None, :]).reshape(depth, D_LRU, D_LRU)
    w["lru_wr"], w["lru_wi"] = bf(dense(gw[..., :blk])), bf(dense(gw[..., blk:]))
    gb = a["lru_gate_b"]
    w["lru_br"] = gb[..., :blk].reshape(depth, 1, D_LRU)
    w["lru_bi"] = gb[..., blk:].reshape(depth, 1, D_LRU)
    w["lru_lambda"], w["conv_b_bias"] = vec(a["lru_lambda"]), vec(a["conv_b_bias"])
    w["conv_a_w"], w["conv_b_w"] = a["conv_a_w"], a["conv_b_w"]

    w["mem_norm"], w["mem_k_gain"] = vec(a["mem_norm"]), vec(a["mem_k_gain"])
    w["mem_w_kv"] = bf(a["mem_w_kv"])
    w["w_branch"], w["w_out"] = bf(a["w_branch"]), bf(a["w_out"])
    return w


def _rope_tables(pos):
    half = ROPE_DIM // 2
    inv = ROPE_THETA ** (-jnp.arange(half, dtype=F32) / half)
    ang = pos.astype(F32)[:, None] * inv[None, :]
    cos, sin = jnp.cos(ang), jnp.sin(ang)
    n = pos.shape[0]
    cos_t = jnp.concatenate([cos, cos, jnp.ones((n, NOPE_DIM), F32), jnp.zeros((n, HEAD_SLOT - QK_DIM), F32)], 1)
    sin_t = jnp.concatenate([sin, sin, jnp.zeros((n, HEAD_SLOT - ROPE_DIM), F32)], axis=1)
    return cos_t, sin_t


def _tile_sizes(T, S, Ts):
    pick = lambda n, opts: next(o for o in opts if n % o == 0)
    return dict(
        tm=pick(np.gcd(T, S), (512, 256, 128)),
        tt=pick(S, (512, 256, 128)),
        ts=pick(Ts, (256, 128, 64, 32, 16, 8)),
        tq=pick(S, (256, 128)),
    )


def kernel(x_prompt, x_sample, cache_kv_latent, cache_k_rope, cache_mem_k, cache_mem_v, state_conv_a, state_conv_b, state_lru_h, page_table, mem_prompt, norm_ffn1, ffn1_w_in, ffn1_w_out, norm_mix, w_in, conv_a_w, conv_b_w, conv_b_bias, lru_gate_w, lru_gate_b, lru_lambda, mla_q_norm, mla_w_uq, mla_kv_norm, mla_w_ukv, mla_q_gain, mla_k_gain, mem_norm, mem_w_kv, mem_k_gain, mem_q_gain, w_branch, w_out, norm_ffn2, ffn2_w_in, ffn2_w_out):
    B, S, D = x_prompt.shape
    DB, DS, _ = x_sample.shape
    depth = norm_ffn1.shape[0]
    page = cache_kv_latent.shape[2]
    past = page_table.shape[1] * page
    Tp, Ts = B * S, DB * DS
    T = Tp + Ts
    ts_ = _tile_sizes(T, S, Ts)
    tm, tt, ts, tq = ts_["tm"], ts_["tt"], ts_["ts"], ts_["tq"]
    assert Tp % ts == 0 and Tp % DS == 0

    w = _prepare_weights(dict(
        norm_ffn1=norm_ffn1, ffn1_w_in=ffn1_w_in, ffn1_w_out=ffn1_w_out, norm_mix=norm_mix, w_in=w_in,
        conv_a_w=conv_a_w, conv_b_w=conv_b_w, conv_b_bias=conv_b_bias, lru_gate_w=lru_gate_w,
        lru_gate_b=lru_gate_b, lru_lambda=lru_lambda, mla_q_norm=mla_q_norm, mla_w_uq=mla_w_uq,
        mla_kv_norm=mla_kv_norm, mla_w_ukv=mla_w_ukv, mla_q_gain=mla_q_gain, mla_k_gain=mla_k_gain,
        mem_norm=mem_norm, mem_w_kv=mem_w_kv, mem_k_gain=mem_k_gain, mem_q_gain=mem_q_gain,
        w_branch=w_branch, w_out=w_out, norm_ffn2=norm_ffn2, ffn2_w_in=ffn2_w_in, ffn2_w_out=ffn2_w_out))

    pos = jnp.concatenate([jnp.tile(jnp.arange(S), B), jnp.tile(past + jnp.arange(DS), DB)])
    cos_t, sin_t = _rope_tables(pos)

    ka, kb = state_conv_a.shape[2], state_conv_b.shape[2]
    sa = jnp.pad(state_conv_a, ((0, 0), (0, 0), (DS - ka, 0), (0, 0))).reshape(depth, Ts, D_CONV)
    sb = jnp.pad(state_conv_b, ((0, 0), (0, 0), (DS - kb, 0), (0, 0))).reshape(depth, Ts, D_LRU)
    h0 = jnp.pad(state_lru_h[:, :, None, :], ((0, 0), (0, 0), (0, DS - 1), (0, 0))).reshape(depth, Ts, D_LRU)
    cmk = cache_mem_k.reshape(depth, DB, N_MEM, MEM_WIDTH)
    cmv = cache_mem_v.reshape(depth, DB, N_MEM, MEM_WIDTH)
    cache_krt = jnp.swapaxes(cache_k_rope, 2, 3).astype(BF16)

    mk_p, mv_p = _memkv(mem_prompt.reshape(B * N_MEM, D), w["mem_norm"], w["mem_w_kv"], w["mem_k_gain"])
    mk_p4 = mk_p.reshape(depth, B, N_MEM, MEM_WIDTH)
    mv_p4 = mv_p.reshape(depth, B, N_MEM, MEM_WIDTH)

    x = jnp.concatenate([x_prompt.reshape(Tp, D), x_sample.reshape(Ts, D)], axis=0)
    lats, krs, p_ca, p_cb, p_h, s_ca, s_cb, s_h = [], [], [], [], [], [], [], []
    n_pages = page_table.shape[1]
    nsub = next(n for n in (4, 2, 1) if n_pages % n == 0)
    nb = next(n for n in (8, 4, 2, 1) if DB % n == 0)
    ch = max(S // 8, tq)
    for l in range(depth):
        x = _ffn(x, w["norm_ffn1"], w["ffn1_g"], w["ffn1_u"], w["ffn1_d"], l, tm)
        P = _proj(x, w["norm_mix"], w["w_in"], l, tm)
        q, lat, kr, k, v, qm = _prep(P, cos_t, sin_t, w, l, tm)
        ya_p, yr_p, ta, tb, th = _convlru_prompt(P, w, l, B, S, tt)
        ya_s, yr_s, va_s, h_s = _convlru_sample(P, sa, sb, h0, w, l, Tp, Ts, DS, ts)
        yc_p = _flash(q, k, v, B, S, tq, ch)
        yc_s = _sattn(page_table, q[Tp:].astype(F32), lat, kr, cache_kv_latent, cache_krt, w, l,
                      Tp, DB, DS, nsub)
        ym_p = _mem_prompt(qm, mk_p4, mv_p4, l, B, S, tm)
        ym_s = _mem_sample(qm[Tp:], cmk, cmv, l, DB, DS, nb)
        ya = jnp.concatenate([ya_p, ya_s], axis=0)
        yr = jnp.concatenate([yr_p, yr_s], axis=0)
        yc = jnp.concatenate([yc_p, yc_s.astype(BF16)], axis=0)
        ym = jnp.concatenate([ym_p, ym_s.astype(BF16)], axis=0)
        x = _merge(x, P, ya, yr, yc, ym, w["w_branch"], w["w_out"], l, tm)
        x = _ffn(x, w["norm_ffn2"], w["ffn2_g"], w["ffn2_u"], w["ffn2_d"], l, tm)

        lats.append(lat)
        krs.append(kr[:, :ROPE_DIM])
        p_ca.append(ta[:, SUBLANES - ka:])
        p_cb.append(tb[:, SUBLANES - kb:])
        p_h.append(th[:, SUBLANES - 1])
        s_ca.append(va_s.reshape(DB, DS, D_CONV)[:, DS - ka:])
        s_cb.append(P[Tp:, P_CONV + 3 * D_CONV:P_CONV + 3 * D_CONV + D_LRU].reshape(DB, DS, D_LRU)[:, DS - kb:].astype(F32))
        s_h.append(h_s.reshape(DB, DS, D_LRU)[:, DS - 1])

    lat_all = jnp.stack(lats)
    kr_all = jnp.stack(krs)
    return (
        x[:Tp].reshape(B, S, D), x[Tp:].reshape(DB, DS, D),
        lat_all[:, :Tp].reshape(depth, B, S, KV_LORA), kr_all[:, :Tp].reshape(depth, B, S, ROPE_DIM),
        jnp.stack(p_ca), jnp.stack(p_cb), jnp.stack(p_h),
        mk_p.reshape(depth, B, N_MEM, MEM_HEADS, MEM_HD), mv_p.reshape(depth, B, N_MEM, MEM_HEADS, MEM_HD),
        lat_all[:, Tp:].reshape(depth, DB, DS, KV_LORA), kr_all[:, Tp:].reshape(depth, DB, DS, ROPE_DIM),
        jnp.stack(s_ca), jnp.stack(s_cb), jnp.stack(s_h),
    )
```

```python
import functools

import numpy as np
import jax
import jax.numpy as jnp
from jax import lax
from jax.experimental import pallas as pl
from jax.experimental.pallas import tpu as pltpu

F32 = jnp.float32
BF16 = jnp.bfloat16

D_FF = 2816
D_CONV = 512
D_LRU = 512
LRU_BLOCKS = 8
LRU_C = 8.0
MLA_HEADS = 8
Q_LORA = 384
KV_LORA = 256
NOPE_DIM = 64
ROPE_DIM = 32
V_DIM = 64
QK_DIM = NOPE_DIM + ROPE_DIM
ROPE_THETA = 10000.0
N_MEM = 256
MEM_HEADS = 4
MEM_HD = 128
MEM_WIDTH = MEM_HEADS * MEM_HD
N_BRANCH = 4
EPS = 1e-6
NEG = -1e30

LANES = 128
SUBLANES = 8
HEAD_SLOT = LANES
VMEM_LIMIT = 56 * 1024 * 1024
LOG2E = 1.4426950408889634
NK = MLA_HEADS * NOPE_DIM
DMA_UNROLL = 8

P_GATES = 0
P_CONV = 4096
P_TAIL = 6144
P_WIDTH = 8192
P_CHUNK = 2048
T_RY, T_CQ, T_CKV, T_KRA, T_KRB, T_MQ = 0, 512, 896, 1152, 1280, 1408


def _dot(a, b):
    return jnp.dot(a, b, preferred_element_type=F32)


def _dot_nt(a, b):
    return lax.dot_general(a, b, (((1,), (1,)), ((), ())), preferred_element_type=F32)


def _rms(x, g):
    return x * lax.rsqrt(jnp.mean(x * x, axis=-1, keepdims=True) + EPS) * g


def _params(*sem):
    return pltpu.CompilerParams(dimension_semantics=sem, vmem_limit_bytes=VMEM_LIMIT)


def _ffn_kernel(x_ref, g_ref, wg_ref, wu_ref, wd_ref, o_ref, xn_ref, acc_ref, *, nf):
    j = pl.program_id(1)

    @pl.when(j == 0)
    def _():
        xn_ref[...] = _rms(x_ref[...], g_ref[...]).astype(BF16)
        acc_ref[...] = jnp.zeros_like(acc_ref)

    xn = xn_ref[...]
    g = _dot(xn, wg_ref[...])
    u = _dot(xn, wu_ref[...])
    h = (g * jax.nn.sigmoid(g) * u).astype(BF16)
    acc_ref[...] += _dot(h, wd_ref[...])

    @pl.when(j == nf - 1)
    def _():
        o_ref[...] = x_ref[...] + 0.5 * acc_ref[...]


def _ffn(x, gain, wg, wu, wd, l, tm):
    T, D = x.shape
    nf = 2
    tf = wg.shape[2] // nf
    return pl.pallas_call(
        functools.partial(_ffn_kernel, nf=nf),
        grid=(T // tm, nf),
        in_specs=[
            pl.BlockSpec((tm, D), lambda i, j: (i, 0)),
            pl.BlockSpec((None, 1, D), lambda i, j: (l, 0, 0)),
            pl.BlockSpec((None, D, tf), lambda i, j: (l, 0, j)),
            pl.BlockSpec((None, D, tf), lambda i, j: (l, 0, j)),
            pl.BlockSpec((None, tf, D), lambda i, j: (l, j, 0)),
        ],
        out_specs=pl.BlockSpec((tm, D), lambda i, j: (i, 0)),
        out_shape=jax.ShapeDtypeStruct((T, D), F32),
        scratch_shapes=[pltpu.VMEM((tm, D), BF16), pltpu.VMEM((tm, D), F32)],
        compiler_params=_params("parallel", "arbitrary"),
        name="ffn",
    )(x, gain, wg, wu, wd)


def _proj_kernel(x_ref, g_ref, w_ref, o_ref, xn_ref, *, n_gate_chunks):
    j = pl.program_id(1)

    @pl.when(j == 0)
    def _():
        xn_ref[...] = _rms(x_ref[...], g_ref[...]).astype(BF16)

    y = _dot(xn_ref[...], w_ref[...])

    @pl.when(j < n_gate_chunks)
    def _():
        o_ref[...] = jax.nn.sigmoid(y).astype(o_ref.dtype)

    @pl.when(j >= n_gate_chunks)
    def _():
        o_ref[...] = y.astype(o_ref.dtype)


def _proj(x, gain, w, l, tm):
    T, D = x.shape
    return pl.pallas_call(
        functools.partial(_proj_kernel, n_gate_chunks=P_CONV // P_CHUNK),
        grid=(T // tm, P_WIDTH // P_CHUNK),
        in_specs=[
            pl.BlockSpec((tm, D), lambda i, j: (i, 0)),
            pl.BlockSpec((None, 1, D), lambda i, j: (l, 0, 0)),
            pl.BlockSpec((None, D, P_CHUNK), lambda i, j: (l, 0, j)),
        ],
        out_specs=pl.BlockSpec((tm, P_CHUNK), lambda i, j: (i, j)),
        out_shape=jax.ShapeDtypeStruct((T, P_WIDTH), BF16),
        scratch_shapes=[pltpu.VMEM((tm, D), BF16)],
        compiler_params=_params("parallel", "arbitrary"),
        name="proj",
    )(x, gain, w)


def _head_norm_store(dst_ref, x, gain, n_heads, inv_width):
    for h in range(n_heads):
        sl = slice(h * HEAD_SLOT, (h + 1) * HEAD_SLOT)
        xh = x[:, sl]
        ms = jnp.sum(xh * xh, axis=-1, keepdims=True) * inv_width
        dst_ref[:, sl] = (xh * lax.rsqrt(ms + EPS) * gain).astype(dst_ref.dtype)


def _prep_kernel(p_ref, cos_ref, sin_ref, qn_ref, wa_ref, wb_ref, qg_ref, kvn_ref, wk_ref, wv_ref,
                 kg_ref, mqg_ref, q_ref, lat_ref, kr_ref, k_ref, v_ref, qm_ref):
    cos = cos_ref[...]
    sin = sin_ref[...]
    n = _rms(p_ref[:, T_CQ:T_CQ + Q_LORA].astype(F32), qn_ref[...]).astype(BF16)
    qa = _dot(n, wa_ref[...])
    qb = _dot(n, wb_ref[...])
    qg = qg_ref[...] * (QK_DIM ** -0.5 * LOG2E)
    for h in range(MLA_HEADS):
        sl = slice(h * HEAD_SLOT, (h + 1) * HEAD_SLOT)
        qh = qa[:, sl] * cos + qb[:, sl] * sin
        ms = jnp.sum(qh * qh, axis=-1, keepdims=True) * (1.0 / QK_DIM)
        q_ref[:, sl] = (qh * lax.rsqrt(ms + EPS) * qg).astype(q_ref.dtype)
    lat = _rms(p_ref[:, T_CKV:T_CKV + KV_LORA].astype(F32), kvn_ref[...])
    lat_ref[...] = lat
    kr = p_ref[:, T_KRA:T_KRA + LANES].astype(F32) * cos + p_ref[:, T_KRB:T_KRB + LANES].astype(F32) * sin
    kr_ref[...] = kr
    latb = lat.astype(BF16)
    kn = _dot(latb, wk_ref[...])
    kg = kg_ref[...]
    for h in range(MLA_HEADS):
        sl = slice(h * HEAD_SLOT, (h + 1) * HEAD_SLOT)
        kh = kn[:, sl] + kr
        ms = jnp.sum(kh * kh, axis=-1, keepdims=True) * (1.0 / QK_DIM)
        k_ref[:, sl] = (kh * lax.rsqrt(ms + EPS) * kg).astype(k_ref.dtype)
    v_ref[...] = _dot(latb, wv_ref[...]).astype(v_ref.dtype)
    _head_norm_store(qm_ref, p_ref[:, T_MQ:T_MQ + MEM_WIDTH].astype(F32), mqg_ref[...] * (MEM_HD ** -0.5),
                     MEM_HEADS, 1.0 / MEM_HD)


def _prep(P, cos, sin, w, l, tm):
    T = P.shape[0]
    HW = MLA_HEADS * HEAD_SLOT
    lay = lambda *shape: pl.BlockSpec((None,) + shape, lambda i: (l,) + (0,) * len(shape))
    row = lambda width: pl.BlockSpec((tm, width), lambda i: (i, 0))
    return pl.pallas_call(
        _prep_kernel,
        grid=(T // tm,),
        in_specs=[
            pl.BlockSpec((tm, P_CHUNK), lambda i: (i, P_TAIL // P_CHUNK)),
            row(LANES), row(LANES),
            lay(1, Q_LORA), lay(Q_LORA, HW), lay(Q_LORA, HW), lay(1, HEAD_SLOT),
            lay(1, KV_LORA), lay(KV_LORA, HW), lay(KV_LORA, MLA_HEADS * V_DIM), lay(1, HEAD_SLOT),
            lay(1, MEM_HD),
        ],
        out_specs=[row(HW), row(KV_LORA), row(LANES), row(HW), row(MLA_HEADS * V_DIM), row(MEM_WIDTH)],
        out_shape=[
            jax.ShapeDtypeStruct((T, HW), BF16),
            jax.ShapeDtypeStruct((T, KV_LORA), F32),
            jax.ShapeDtypeStruct((T, LANES), F32),
            jax.ShapeDtypeStruct((T, HW), BF16),
            jax.ShapeDtypeStruct((T, MLA_HEADS * V_DIM), BF16),
            jax.ShapeDtypeStruct((T, MEM_WIDTH), BF16),
        ],
        compiler_params=_params("parallel"),
        name="prep",
    )(P, cos, sin, w["q_norm"], w["uq_a"], w["uq_b"], w["q_gain"], w["kv_norm"], w["wk_slot"],
      w["wv"], w["k_gain"], w["mq_gain"])


def _lru_coeffs(xr, wr, wi, br, bi, lam):
    xb = xr.astype(BF16)
    r = jax.nn.sigmoid(_dot(xb, wr) + br)
    i = jax.nn.sigmoid(_dot(xb, wi) + bi)
    log_a = -LRU_C * r * jax.nn.softplus(-lam)
    a = jnp.exp(log_a)
    b = jnp.sqrt(1.0 - a * a) * (i * xr)
    return a, b


def _scan(a, b, pos, seg):
    d = 1
    while d < seg:
        keep = pos >= d
        a_s = jnp.where(keep, pltpu.roll(a, d, 0), 1.0)
        b_s = jnp.where(keep, pltpu.roll(b, d, 0), 0.0)
        b = a * b_s + b
        a = a * a_s
        d *= 2
    return a, b


def _conv_taps(x, shifted, w, bias=None):
    K = w.shape[0]
    y = shifted(K - 1) * w[0:1]
    for k in range(1, K - 1):
        y = y + shifted(K - 1 - k) * w[k:k + 1]
    y = y + x * w[K - 1:K]
    if bias is not None:
        y = y + bias
    return y


def _convlru_prompt_kernel(p_ref, ry_ref, caw_ref, cbw_ref, cbb_ref, wr_ref, wi_ref, br_ref, bi_ref,
                           lam_ref, ya_ref, yr_ref, ta_ref, tb_ref, th_ref, pa_ref, pb_ref, h_ref,
                           as_ref, bs_ref, hs_ref):
    tt = p_ref.shape[0]

    @pl.when(pl.program_id(1) == 0)
    def _():
        pa_ref[...] = jnp.zeros_like(pa_ref)
        pb_ref[...] = jnp.zeros_like(pb_ref)
        h_ref[...] = jnp.zeros_like(h_ref)

    row = lax.broadcasted_iota(jnp.int32, (tt, 1), 0)
    row8 = lax.broadcasted_iota(jnp.int32, (SUBLANES, 1), 0)

    def delayed(x, prev8):
        def shifted(s):
            r = pltpu.roll(x, s, 0)
            first = jnp.where(row8 < s, pltpu.roll(prev8, s, 0), r[:SUBLANES])
            return jnp.concatenate([first, r[SUBLANES:]], axis=0)
        return shifted

    a_b = p_ref[:, 0:D_CONV].astype(F32)
    v = p_ref[:, D_CONV:2 * D_CONV].astype(F32) * p_ref[:, 2 * D_CONV:3 * D_CONV].astype(F32)
    u = _conv_taps(v, delayed(v, pa_ref[...]), caw_ref[...])
    ya_ref[...] = (a_b * u).astype(ya_ref.dtype)

    rx = p_ref[:, 3 * D_CONV:3 * D_CONV + D_LRU].astype(F32)
    xr = _conv_taps(rx, delayed(rx, pb_ref[...]), cbw_ref[...], cbb_ref[...])
    a, b = _lru_coeffs(xr, wr_ref[...], wi_ref[...], br_ref[...], bi_ref[...], lam_ref[...])
    as_ref[...], bs_ref[...] = _scan(a, b, row % SUBLANES, SUBLANES)

    def carry_group(g, h_in):
        r = pl.ds(pl.multiple_of(g * SUBLANES, SUBLANES), SUBLANES)
        hg = bs_ref[r, :] + as_ref[r, :] * h_in
        hs_ref[r, :] = hg
        return hg[SUBLANES - 1:SUBLANES, :]

    lax.fori_loop(0, tt // SUBLANES, carry_group, h_ref[...], unroll=8)
    h = hs_ref[...]
    yr_ref[...] = (h * jax.nn.gelu(ry_ref[...].astype(F32))).astype(yr_ref.dtype)

    tail = slice(tt - SUBLANES, tt)
    ta_ref[...] = v[tail]
    tb_ref[...] = rx[tail]
    th_ref[...] = h[tail]
    pa_ref[...] = v[tail]
    pb_ref[...] = rx[tail]
    h_ref[...] = h[tt - 1:tt]


def _convlru_prompt(P, w, l, B, S, tt):
    nt = S // tt
    cb = P_CHUNK // D_CONV
    lay = lambda *shape: pl.BlockSpec((None,) + shape, lambda b, t: (l,) + (0,) * len(shape))
    seq = pl.BlockSpec((tt, D_CONV), lambda b, t: (b * nt + t, 0))
    tail = pl.BlockSpec((None, SUBLANES, D_CONV), lambda b, t: (b, 0, 0))
    tail_shape = jax.ShapeDtypeStruct((B, SUBLANES, D_CONV), F32)
    return pl.pallas_call(
        _convlru_prompt_kernel,
        grid=(B, nt),
        in_specs=[
            pl.BlockSpec((tt, P_CHUNK), lambda b, t: (b * nt + t, P_CONV // P_CHUNK)),
            pl.BlockSpec((tt, D_LRU), lambda b, t: (b * nt + t, P_TAIL // D_LRU)),
            lay(3, D_CONV), lay(4, D_LRU), lay(1, D_LRU), lay(D_LRU, D_LRU), lay(D_LRU, D_LRU),
            lay(1, D_LRU), lay(1, D_LRU), lay(1, D_LRU),
        ],
        out_specs=[seq, seq, tail, tail, tail],
        out_shape=[
            jax.ShapeDtypeStruct((B * S, D_CONV), BF16),
            jax.ShapeDtypeStruct((B * S, D_LRU), BF16),
            tail_shape, tail_shape, tail_shape,
        ],
        scratch_shapes=[pltpu.VMEM((SUBLANES, D_CONV), F32), pltpu.VMEM((SUBLANES, D_LRU), F32),
                        pltpu.VMEM((1, D_LRU), F32)] + [pltpu.VMEM((tt, D_LRU), F32)] * 3,
        compiler_params=_params("parallel", "arbitrary"),
        name="convlru_prompt",
    )(P, P, w["conv_a_w"], w["conv_b_w"], w["conv_b_bias"], w["lru_wr"], w["lru_wi"], w["lru_br"],
      w["lru_bi"], w["lru_lambda"])


def _convlru_sample_kernel(p_ref, ry_ref, sa_ref, sb_ref, h0_ref, caw_ref, cbw_ref, cbb_ref, wr_ref,
                           wi_ref, br_ref, bi_ref, lam_ref, ya_ref, yr_ref, va_ref, h_ref, *, ds):
    ts = p_ref.shape[0]
    pos = lax.broadcasted_iota(jnp.int32, (ts, 1), 0) % ds

    def delayed(x, state):
        def shifted(s):
            return jnp.where(pos < s, pltpu.roll(state, ts - ds + s, 0), pltpu.roll(x, s, 0))
        return shifted

    a_b = p_ref[:, 0:D_CONV].astype(F32)
    v = p_ref[:, D_CONV:2 * D_CONV].astype(F32) * p_ref[:, 2 * D_CONV:3 * D_CONV].astype(F32)
    u = _conv_taps(v, delayed(v, sa_ref[...]), caw_ref[...])
    ya_ref[...] = (a_b * u).astype(ya_ref.dtype)
    va_ref[...] = v

    rx = p_ref[:, 3 * D_CONV:3 * D_CONV + D_LRU].astype(F32)
    xr = _conv_taps(rx, delayed(rx, sb_ref[...]), cbw_ref[...], cbb_ref[...])
    a, b = _lru_coeffs(xr, wr_ref[...], wi_ref[...], br_ref[...], bi_ref[...], lam_ref[...])
    b = b + a * h0_ref[...]
    _, h = _scan(a, b, pos, ds)
    h_ref[...] = h
    yr_ref[...] = (h * jax.nn.gelu(ry_ref[...].astype(F32))).astype(yr_ref.dtype)


def _convlru_sample(P, sa, sb, h0, w, l, row0, Ts, ds, ts):
    assert ds == SUBLANES
    r0 = row0 // ts
    lay = lambda *shape: pl.BlockSpec((None,) + shape, lambda i: (l,) + (0,) * len(shape))
    st = pl.BlockSpec((None, ts, D_CONV), lambda i: (l, i, 0))
    seq = pl.BlockSpec((ts, D_CONV), lambda i: (i, 0))
    return pl.pallas_call(
        functools.partial(_convlru_sample_kernel, ds=ds),
        grid=(Ts // ts,),
        in_specs=[
            pl.BlockSpec((ts, P_CHUNK), lambda i: (r0 + i, P_CONV // P_CHUNK)),
            pl.BlockSpec((ts, D_LRU), lambda i: (r0 + i, P_TAIL // D_LRU)),
            st, st, st,
            lay(3, D_CONV), lay(4, D_LRU), lay(1, D_LRU), lay(D_LRU, D_LRU), lay(D_LRU, D_LRU),
            lay(1, D_LRU), lay(1, D_LRU), lay(1, D_LRU),
        ],
        out_specs=[seq, seq, seq, seq],
        out_shape=[
            jax.ShapeDtypeStruct((Ts, D_CONV), BF16),
            jax.ShapeDtypeStruct((Ts, D_LRU), BF16),
            jax.ShapeDtypeStruct((Ts, D_CONV), F32),
            jax.ShapeDtypeStruct((Ts, D_LRU), F32),
        ],
        compiler_params=_params("parallel"),
        name="convlru_sample",
    )(P, P, sa, sb, h0, w["conv_a_w"], w["conv_b_w"], w["conv_b_bias"], w["lru_wr"], w["lru_wi"],
      w["lru_br"], w["lru_bi"], w["lru_lambda"])


def _flash_kernel(q_ref, k_ref, v_ref, o_ref, *, tq, ch):
    S = k_ref.shape[0]
    qi = pl.program_id(2)
    n_chunks = (qi * tq + tq + ch - 1) // ch
    first_head = lax.broadcasted_iota(jnp.int32, (1, LANES), 1) < V_DIM
    qpos = qi * tq + lax.broadcasted_iota(jnp.int32, (tq, 1), 0)

    def body(n):
        L = n * ch
        v = v_ref[0:L, :]
        one = jnp.ones_like(v)
        vext = (jnp.where(first_head, v, one), jnp.where(first_head, one, v))
        mask = lax.broadcasted_iota(jnp.int32, (1, L), 1) <= qpos
        outs = []
        for hh in range(2):
            sl = slice(hh * HEAD_SLOT, (hh + 1) * HEAD_SLOT)
            s = jnp.where(mask, _dot_nt(q_ref[:, sl], k_ref[0:L, sl]), NEG)
            m = jnp.max(s, axis=-1, keepdims=True)
            a = _dot(jnp.exp2(s - m).astype(BF16), vext[hh])
            outs.append(a / pltpu.roll(a, V_DIM, 1))
        o_ref[...] = jnp.where(first_head, outs[0], outs[1]).astype(o_ref.dtype)

    for n in range(1, S // ch + 1):
        pl.when(n_chunks == n)(functools.partial(body, n))


def _flash(q, k, v, B, S, tq, ch):
    nq = S // tq
    hp = MLA_HEADS // 2
    return pl.pallas_call(
        functools.partial(_flash_kernel, tq=tq, ch=ch),
        grid=(B, hp, nq),
        in_specs=[
            pl.BlockSpec((tq, 2 * HEAD_SLOT), lambda b, h, i: (b * nq + i, h)),
            pl.BlockSpec((S, 2 * HEAD_SLOT), lambda b, h, i: (b, h)),
            pl.BlockSpec((S, 2 * V_DIM), lambda b, h, i: (b, h)),
        ],
        out_specs=pl.BlockSpec((tq, 2 * V_DIM), lambda b, h, i: (b * nq + i, h)),
        out_shape=jax.ShapeDtypeStruct((B * S, MLA_HEADS * V_DIM), BF16),
        compiler_params=_params("parallel", "parallel", "arbitrary"),
        name="flash",
    )(q, k, v)


def _sattn_kernel(pt_ref, q_ref, latn_ref, krn_ref, wkn_ref, wkt_ref, wv_ref, kg_ref, clat_ref, ckrt_ref,
                  o_ref, a_ref, lat_st, kr_st, sem, *, l, n_pages, page, ds, nsub):
    b = pl.program_id(0)
    nb = pl.num_programs(0)
    R = MLA_HEADS * ds
    n = n_pages * page
    sub = n // nsub

    def page_copies(seq, slot, i):
        pid = pt_ref[seq, i]
        dst = pl.ds(pl.multiple_of(i * page, page), page)
        return (pltpu.make_async_copy(clat_ref.at[l, pid], lat_st.at[slot, dst, :], sem.at[0, slot]),
                pltpu.make_async_copy(ckrt_ref.at[l, pid], kr_st.at[slot, :, dst], sem.at[1, slot]))

    def start_all(seq, slot):
        def body(i, c):
            for cp in page_copies(seq, slot, i):
                cp.start()
            return c
        lax.fori_loop(0, n_pages, body, 0, unroll=DMA_UNROLL)

    def wait_all(seq, slot):
        def body(i, c):
            for cp in page_copies(seq, slot, i):
                cp.wait()
            return c
        lax.fori_loop(0, n_pages, body, 0, unroll=DMA_UNROLL)

    slot = b % 2

    @pl.when(b == 0)
    def _():
        start_all(0, 0)

    @pl.when(b + 1 < nb)
    def _():
        start_all(b + 1, 1 - slot)

    qk = q_ref[...] * kg_ref[...]
    qa, qp = [], []
    for h in range(MLA_HEADS):
        qh = qk[:, h * HEAD_SLOT:(h + 1) * HEAD_SLOT]
        qa.append(_dot(qh.astype(BF16), wkt_ref[h]))
        qp.append(qh[:, 0:ROPE_DIM])
    a_ref[0:NK, :] = wkn_ref[...]
    a_ref[NK:NK + R, :] = jnp.concatenate(qa, axis=0).astype(BF16)
    qpe = jnp.concatenate(qp, axis=0).astype(BF16)
    ones_pe = jnp.ones((MLA_HEADS, ROPE_DIM), BF16)

    def update(carry, L, s_pe, ssq_pe, mask):
        m_prev, l_prev, acc = carry
        u = _dot_nt(a_ref[...], L)
        kn = u[0:NK].reshape(MLA_HEADS, NOPE_DIM, -1)
        ssq = jnp.sum(kn * kn, axis=1) + ssq_pe
        rinv = lax.rsqrt(ssq * (1.0 / QK_DIM) + EPS)
        s = ((u[NK:NK + R] + s_pe).reshape(MLA_HEADS, ds, -1) * rinv[:, None, :]).reshape(R, -1)
        if mask is not None:
            s = jnp.where(mask, s, NEG)
        m_new = jnp.maximum(m_prev, jnp.max(s, axis=-1, keepdims=True))
        al = jnp.exp2(m_prev - m_new)
        pr = jnp.exp2(s - m_new)
        l_new = al * l_prev + jnp.sum(pr, axis=-1, keepdims=True)
        return m_new, l_new, al * acc + _dot(pr.astype(BF16), L)

    Ln = jnp.concatenate([latn_ref[...], jnp.zeros((page - ds, KV_LORA), F32)], axis=0).astype(BF16)
    Rn = jnp.concatenate([krn_ref[:, 0:ROPE_DIM], jnp.zeros((page - ds, ROPE_DIM), F32)], axis=0)
    key = lax.broadcasted_iota(jnp.int32, (R, page), 1)
    qry = lax.broadcasted_iota(jnp.int32, (R, page), 0) % ds
    carry = (jnp.full((R, 1), NEG, F32), jnp.zeros((R, 1), F32), jnp.zeros((R, KV_LORA), F32))
    carry = update(carry, Ln, _dot_nt(qpe, Rn.astype(BF16)), _dot_nt(ones_pe, (Rn * Rn).astype(BF16)),
                   key <= qry)

    wait_all(b, slot)
    for c in range(nsub):
        L = lat_st[slot, c * sub:(c + 1) * sub, :].astype(BF16)
        rt = kr_st[slot, :, c * sub:(c + 1) * sub]
        rf = rt.astype(F32)
        carry = update(carry, L, _dot(qpe, rt), _dot(ones_pe, (rf * rf).astype(BF16)), None)

    _, l_fin, acc = carry
    o_lat = (acc / l_fin).astype(BF16)
    y = _dot(o_lat, wv_ref[...])
    vrow = lax.broadcasted_iota(jnp.int32, (R, MLA_HEADS * V_DIM), 0) // ds
    vcol = lax.broadcasted_iota(jnp.int32, (R, MLA_HEADS * V_DIM), 1) // V_DIM
    y = jnp.where(vrow == vcol, y, 0.0)
    o_ref[...] = jnp.sum(y.reshape(MLA_HEADS, ds, MLA_HEADS * V_DIM), axis=0)


def _sattn(page_table, q_s, lat, kr, cache_lat, cache_krt, w, l, row0, DB, ds, nsub):
    n_pages = page_table.shape[1]
    page = cache_lat.shape[2]
    assert ds == SUBLANES and MLA_HEADS == SUBLANES and (n_pages * page) % nsub == 0
    HW = MLA_HEADS * HEAD_SLOT
    R = MLA_HEADS * ds
    r0 = row0 // ds
    lay = lambda *shape: pl.BlockSpec((None,) + shape, lambda b, pt: (l,) + (0,) * len(shape))
    grid_spec = pltpu.PrefetchScalarGridSpec(
        num_scalar_prefetch=1,
        grid=(DB,),
        in_specs=[
            pl.BlockSpec((ds, HW), lambda b, pt: (b, 0)),
            pl.BlockSpec((ds, KV_LORA), lambda b, pt: (r0 + b, 0)),
            pl.BlockSpec((ds, LANES), lambda b, pt: (r0 + b, 0)),
            lay(NK, KV_LORA), lay(MLA_HEADS, HEAD_SLOT, KV_LORA), lay(KV_LORA, MLA_HEADS * V_DIM), lay(1, HW),
            pl.BlockSpec(memory_space=pl.ANY), pl.BlockSpec(memory_space=pl.ANY),
        ],
        out_specs=pl.BlockSpec((ds, MLA_HEADS * V_DIM), lambda b, pt: (b, 0)),
        scratch_shapes=[
            pltpu.VMEM((NK + R, KV_LORA), BF16),
            pltpu.VMEM((2, n_pages * page, KV_LORA), F32),
            pltpu.VMEM((2, ROPE_DIM, n_pages * page), BF16),
            pltpu.SemaphoreType.DMA((2, 2)),
        ],
    )
    return pl.pallas_call(
        functools.partial(_sattn_kernel, l=l, n_pages=n_pages, page=page, ds=ds, nsub=nsub),
        grid_spec=grid_spec,
        out_shape=jax.ShapeDtypeStruct((DB * ds, MLA_HEADS * V_DIM), F32),
        compiler_params=_params("arbitrary"),
        name="sattn",
    )(page_table, q_s, lat, kr, w["wk_nt"], w["wk_t"], w["wv"], w["k_gain_tiled"], cache_lat, cache_krt)


def _softmax_rows(s):
    m = jnp.max(s, axis=-1, keepdims=True)
    p = jnp.exp(s - m)
    return p / jnp.sum(p, axis=-1, keepdims=True)


def _mem_prompt_kernel(q_ref, k_ref, v_ref, o_ref):
    for h in range(MEM_HEADS):
        sl = slice(h * MEM_HD, (h + 1) * MEM_HD)
        p = _softmax_rows(_dot_nt(q_ref[:, sl], k_ref[:, sl].astype(BF16)))
        o_ref[:, sl] = _dot(p.astype(BF16), v_ref[:, sl].astype(BF16)).astype(o_ref.dtype)


def _mem_prompt(qm, mk, mv, l, B, S, tm):
    nt = S // tm
    kv = pl.BlockSpec((None, None, N_MEM, MEM_WIDTH), lambda b, t: (l, b, 0, 0))
    seq = pl.BlockSpec((tm, MEM_WIDTH), lambda b, t: (b * nt + t, 0))
    return pl.pallas_call(
        _mem_prompt_kernel,
        grid=(B, nt),
        in_specs=[seq, kv, kv],
        out_specs=seq,
        out_shape=jax.ShapeDtypeStruct((B * S, MEM_WIDTH), BF16),
        compiler_params=_params("parallel", "parallel"),
        name="mem_prompt",
    )(qm, mk, mv)


def _mem_sample_kernel(q_ref, k_ref, v_ref, o_ref, *, nb, ds):
    q = q_ref[...].reshape(nb, ds, MEM_WIDTH)
    outs = []
    for h in range(MEM_HEADS):
        sl = slice(h * MEM_HD, (h + 1) * MEM_HD)
        s = jnp.einsum('bqd,bkd->bqk', q[:, :, sl], k_ref[:, :, sl].astype(BF16),
                       preferred_element_type=F32)
        p = _softmax_rows(s)
        outs.append(jnp.einsum('bqk,bkd->bqd', p.astype(BF16), v_ref[:, :, sl].astype(BF16),
                               preferred_element_type=F32))
    o_ref[...] = jnp.concatenate(outs, axis=-1).reshape(nb * ds, MEM_WIDTH)


def _mem_sample(qm_s, ck, cv, l, DB, ds, nb):
    kv = pl.BlockSpec((None, nb, N_MEM, MEM_WIDTH), lambda i: (l, i, 0, 0))
    seq = pl.BlockSpec((nb * ds, MEM_WIDTH), lambda i: (i, 0))
    return pl.pallas_call(
        functools.partial(_mem_sample_kernel, nb=nb, ds=ds),
        grid=(DB // nb,),
        in_specs=[seq, kv, kv],
        out_specs=seq,
        out_shape=jax.ShapeDtypeStruct((DB * ds, MEM_WIDTH), F32),
        compiler_params=_params("parallel"),
        name="mem_sample",
    )(qm_s, ck, cv)


def _memkv_kernel(x_ref, n_ref, w_ref, kg_ref, k_ref, v_ref):
    kv = _dot(_rms(x_ref[...], n_ref[...]).astype(BF16), w_ref[...])
    _head_norm_store(k_ref, kv[:, :MEM_WIDTH], kg_ref[...], MEM_HEADS, 1.0 / MEM_HD)
    v_ref[...] = kv[:, MEM_WIDTH:]


def _memkv(mem, mem_norm, w_kv, k_gain):
    depth = w_kv.shape[0]
    R, D = mem.shape
    lay = lambda *shape: pl.BlockSpec((None,) + shape, lambda l: (l,) + (0,) * len(shape))
    out = jax.ShapeDtypeStruct((depth, R, MEM_WIDTH), F32)
    return pl.pallas_call(
        _memkv_kernel,
        grid=(depth,),
        in_specs=[pl.BlockSpec((R, D), lambda l: (0, 0)), lay(1, D), lay(D, 2 * MEM_WIDTH), lay(1, MEM_HD)],
        out_specs=[lay(R, MEM_WIDTH), lay(R, MEM_WIDTH)],
        out_shape=[out, out],
        compiler_params=_params("parallel"),
        name="memkv",
    )(mem, mem_norm, w_kv, k_gain)


def _merge_kernel(x_ref, sg_ref, yap_ref, yrp_ref, ycp_ref, ymp_ref, yas_ref, yrs_ref, ycs_ref, yms_ref,
                  wb_ref, wo_ref, o_ref, *, n_prompt_tiles):
    D = x_ref.shape[1]

    def run(y_refs):
        merged = None
        for bi, y_ref in enumerate(y_refs):
            width = y_ref.shape[1]
            t = sg_ref[:, bi * D:(bi + 1) * D].astype(F32) * _dot(
                y_ref[...].astype(BF16), wb_ref[bi * width:(bi + 1) * width, :])
            merged = t if merged is None else merged + t
        o_ref[...] = x_ref[...] + _dot(merged.astype(BF16), wo_ref[...])

    is_prompt = pl.program_id(0) < n_prompt_tiles
    pl.when(is_prompt)(lambda: run((yap_ref, yrp_ref, ycp_ref, ymp_ref)))
    pl.when(jnp.logical_not(is_prompt))(lambda: run((yas_ref, yrs_ref, ycs_ref, yms_ref)))


def _merge(x, P, y_prompt, y_sample, wb, wo, l, tm):
    T, D = x.shape
    npt = y_prompt[0].shape[0] // tm
    row = lambda width: pl.BlockSpec((tm, width), lambda i: (i, 0))
    prow = lambda y: pl.BlockSpec((tm, y.shape[1]), lambda i: (jnp.minimum(i, npt - 1), 0))
    srow = lambda y: pl.BlockSpec((tm, y.shape[1]), lambda i: (jnp.maximum(i - npt, 0), 0))
    lay = lambda *shape: pl.BlockSpec((None,) + shape, lambda i: (l,) + (0,) * len(shape))
    return pl.pallas_call(
        functools.partial(_merge_kernel, n_prompt_tiles=npt),
        grid=(T // tm,),
        in_specs=[row(D), row(N_BRANCH * D)] + [prow(y) for y in y_prompt] + [srow(y) for y in y_sample]
                 + [lay(wb.shape[1], D), lay(D, D)],
        out_specs=row(D),
        out_shape=jax.ShapeDtypeStruct((T, D), F32),
        compiler_params=_params("parallel"),
        name="merge",
    )(x, P, *y_prompt, *y_sample, wb, wo)


def _prepare_weights(a):
    depth, D = a["norm_ffn1"].shape
    bf = lambda t: t.astype(BF16)
    vec = lambda t: t.reshape(depth, 1, -1)
    w = {}
    for name in ("ffn1", "ffn2"):
        w_in = a[name + "_w_in"]
        w[name + "_g"] = bf(w_in[:, :, :D_FF])
        w[name + "_u"] = bf(w_in[:, :, D_FF:])
        w[name + "_d"] = bf(a[name + "_w_out"])
    w["norm_ffn1"], w["norm_ffn2"], w["norm_mix"] = vec(a["norm_ffn1"]), vec(a["norm_ffn2"]), vec(a["norm_mix"])

    splits = (D_CONV, D_CONV, D_CONV, D_LRU, D_LRU, Q_LORA, KV_LORA, ROPE_DIM, MEM_WIDTH) + (D,) * N_BRANCH
    off = np.concatenate([[0], np.cumsum(splits)])
    w_in = a["w_in"]
    piece = lambda i: w_in[:, :, off[i]:off[i + 1]]
    zeros = lambda n: jnp.zeros((depth, D, n), w_in.dtype)
    half = ROPE_DIM // 2
    c_kr = piece(7)
    kr_a = jnp.concatenate([c_kr, zeros(LANES - ROPE_DIM)], axis=-1)
    kr_b = jnp.concatenate([-c_kr[..., half:], c_kr[..., :half], zeros(LANES - ROPE_DIM)], axis=-1)
    w["w_in"] = bf(jnp.concatenate(
        [w_in[:, :, off[9]:]] + [piece(i) for i in range(7)] + [kr_a, kr_b, piece(8), zeros(LANES)], axis=-1))
    assert w["w_in"].shape[-1] == P_WIDTH

    def slot_gain(g):
        return jnp.concatenate([g[:, NOPE_DIM:], g[:, :NOPE_DIM],
                                jnp.zeros((depth, HEAD_SLOT - QK_DIM), g.dtype)], axis=-1)[:, None, :]

    uq = a["mla_w_uq"].reshape(depth, Q_LORA, MLA_HEADS, QK_DIM)
    nope, pe = uq[..., :NOPE_DIM], uq[..., NOPE_DIM:]
    z = lambda n: jnp.zeros((depth, Q_LORA, MLA_HEADS, n), uq.dtype)
    w["uq_a"] = bf(jnp.concatenate([pe, nope, z(HEAD_SLOT - QK_DIM)], axis=-1).reshape(depth, Q_LORA, -1))
    w["uq_b"] = bf(jnp.concatenate([-pe[..., half:], pe[..., :half], z(HEAD_SLOT - ROPE_DIM)],
                                   axis=-1).reshape(depth, Q_LORA, -1))
    w["q_norm"], w["kv_norm"] = vec(a["mla_q_norm"]), vec(a["mla_kv_norm"])
    w["q_gain"] = slot_gain(a["mla_q_gain"])
    w["k_gain"] = slot_gain(a["mla_k_gain"])
    w["k_gain_tiled"] = jnp.tile(w["k_gain"], (1, 1, MLA_HEADS))
    w["mq_gain"] = vec(a["mem_q_gain"])

    ukv = a["mla_w_ukv"].reshape(depth, KV_LORA, MLA_HEADS, NOPE_DIM + V_DIM)
    k_nope, vv = ukv[..., :NOPE_DIM], ukv[..., NOPE_DIM:]
    zk = lambda n: jnp.zeros((depth, KV_LORA, MLA_HEADS, n), ukv.dtype)
    wk_slot = jnp.concatenate([zk(ROPE_DIM), k_nope, zk(HEAD_SLOT - QK_DIM)], axis=-1)
    w["wk_slot"] = bf(wk_slot.reshape(depth, KV_LORA, -1))
    w["wk_t"] = bf(wk_slot.transpose(0, 2, 3, 1))
    w["wk_nt"] = bf(k_nope.reshape(depth, KV_LORA, -1).transpose(0, 2, 1))
    w["wv"] = bf(vv.reshape(depth, KV_LORA, -1))

    gw = a["lru_gate_w"]
    blk = D_LRU // LRU_BLOCKS
    eye = jnp.eye(LRU_BLOCKS, dtype=gw.dtype)[None, :, None, :, None]
    dense = lambda part: (eye * part[:, :, :, None, :]).reshape(depth, D_LRU, D_LRU)
    w["lru_wr"], w["lru_wi"] = bf(dense(gw[..., :blk])), bf(dense(gw[..., blk:]))
    gb = a["lru_gate_b"]
    w["lru_br"] = gb[..., :blk].reshape(depth, 1, D_LRU)
    w["lru_bi"] = gb[..., blk:].reshape(depth, 1, D_LRU)
    w["lru_lambda"], w["conv_b_bias"] = vec(a["lru_lambda"]), vec(a["conv_b_bias"])
    w["conv_a_w"], w["conv_b_w"] = a["conv_a_w"], a["conv_b_w"]

    w["mem_norm"], w["mem_k_gain"] = vec(a["mem_norm"]), vec(a["mem_k_gain"])
    w["mem_w_kv"] = bf(a["mem_w_kv"])
    w["w_branch"], w["w_out"] = bf(a["w_branch"]), bf(a["w_out"])
    return w


def _rope_tables(pos):
    half = ROPE_DIM // 2
    inv = ROPE_THETA ** (-jnp.arange(half, dtype=F32) / half)
    ang = pos.astype(F32)[:, None] * inv[None, :]
    cos, sin = jnp.cos(ang), jnp.sin(ang)
    n = pos.shape[0]
    cos_t = jnp.concatenate([cos, cos, jnp.ones((n, NOPE_DIM), F32), jnp.zeros((n, HEAD_SLOT - QK_DIM), F32)], 1)
    sin_t = jnp.concatenate([sin, sin, jnp.zeros((n, HEAD_SLOT - ROPE_DIM), F32)], axis=1)
    return cos_t, sin_t


def _tile_sizes(T, S, Ts):
    pick = lambda n, opts: next(o for o in opts if n % o == 0)
    return dict(
        tm=pick(np.gcd(T, S), (512, 256, 128)),
        tt=pick(S, (512, 256, 128)),
        ts=pick(Ts, (256, 128, 64, 32, 16, 8)),
        tq=pick(S, (256, 128)),
    )


def kernel(x_prompt, x_sample, cache_kv_latent, cache_k_rope, cache_mem_k, cache_mem_v, state_conv_a, state_conv_b, state_lru_h, page_table, mem_prompt, norm_ffn1, ffn1_w_in, ffn1_w_out, norm_mix, w_in, conv_a_w, conv_b_w, conv_b_bias, lru_gate_w, lru_gate_b, lru_lambda, mla_q_norm, mla_w_uq, mla_kv_norm, mla_w_ukv, mla_q_gain, mla_k_gain, mem_norm, mem_w_kv, mem_k_gain, mem_q_gain, w_branch, w_out, norm_ffn2, ffn2_w_in, ffn2_w_out):
    B, S, D = x_prompt.shape
    DB, DS, _ = x_sample.shape
    depth = norm_ffn1.shape[0]
    page = cache_kv_latent.shape[2]
    past = page_table.shape[1] * page
    Tp, Ts = B * S, DB * DS
    T = Tp + Ts
    ts_ = _tile_sizes(T, S, Ts)
    tm, tt, ts, tq = ts_["tm"], ts_["tt"], ts_["ts"], ts_["tq"]
    assert Tp % ts == 0 and Tp % DS == 0

    w = _prepare_weights(dict(
        norm_ffn1=norm_ffn1, ffn1_w_in=ffn1_w_in, ffn1_w_out=ffn1_w_out, norm_mix=norm_mix, w_in=w_in,
        conv_a_w=conv_a_w, conv_b_w=conv_b_w, conv_b_bias=conv_b_bias, lru_gate_w=lru_gate_w,
        lru_gate_b=lru_gate_b, lru_lambda=lru_lambda, mla_q_norm=mla_q_norm, mla_w_uq=mla_w_uq,
        mla_kv_norm=mla_kv_norm, mla_w_ukv=mla_w_ukv, mla_q_gain=mla_q_gain, mla_k_gain=mla_k_gain,
        mem_norm=mem_norm, mem_w_kv=mem_w_kv, mem_k_gain=mem_k_gain, mem_q_gain=mem_q_gain,
        w_branch=w_branch, w_out=w_out, norm_ffn2=norm_ffn2, ffn2_w_in=ffn2_w_in, ffn2_w_out=ffn2_w_out))

    pos = jnp.concatenate([jnp.tile(jnp.arange(S), B), jnp.tile(past + jnp.arange(DS), DB)])
    cos_t, sin_t = _rope_tables(pos)

    ka, kb = state_conv_a.shape[2], state_conv_b.shape[2]
    sa = jnp.pad(state_conv_a, ((0, 0), (0, 0), (DS - ka, 0), (0, 0))).reshape(depth, Ts, D_CONV)
    sb = jnp.pad(state_conv_b, ((0, 0), (0, 0), (DS - kb, 0), (0, 0))).reshape(depth, Ts, D_LRU)
    h0 = jnp.pad(state_lru_h[:, :, None, :], ((0, 0), (0, 0), (0, DS - 1), (0, 0))).reshape(depth, Ts, D_LRU)
    cmk = cache_mem_k.reshape(depth, DB, N_MEM, MEM_WIDTH).astype(BF16)
    cmv = cache_mem_v.reshape(depth, DB, N_MEM, MEM_WIDTH).astype(BF16)
    cache_krt = jnp.swapaxes(cache_k_rope, 2, 3).astype(BF16)

    mk_p, mv_p = _memkv(mem_prompt.reshape(B * N_MEM, D), w["mem_norm"], w["mem_w_kv"], w["mem_k_gain"])
    mk_p4 = mk_p.reshape(depth, B, N_MEM, MEM_WIDTH)
    mv_p4 = mv_p.reshape(depth, B, N_MEM, MEM_WIDTH)

    x = jnp.concatenate([x_prompt.reshape(Tp, D), x_sample.reshape(Ts, D)], axis=0)
    p_lat, p_kr, p_ca, p_cb, p_h, s_lat, s_kr, s_ca, s_cb, s_h = ([] for _ in range(10))
    n_pages = page_table.shape[1]
    nsub = next(n for n in (4, 2, 1) if n_pages % n == 0)
    nb = next(n for n in (8, 4, 2, 1) if DB % n == 0)
    ch = max(S // 16, tq)
    for l in range(depth):
        x = _ffn(x, w["norm_ffn1"], w["ffn1_g"], w["ffn1_u"], w["ffn1_d"], l, tm)
        P = _proj(x, w["norm_mix"], w["w_in"], l, tm)
        q, lat, kr, k, v, qm = _prep(P, cos_t, sin_t, w, l, tm)
        ya_p, yr_p, ta, tb, th = _convlru_prompt(P, w, l, B, S, tt)
        ya_s, yr_s, va_s, h_s = _convlru_sample(P, sa, sb, h0, w, l, Tp, Ts, DS, ts)
        yc_p = _flash(q, k, v, B, S, tq, ch)
        yc_s = _sattn(page_table, q[Tp:].astype(F32), lat, kr, cache_kv_latent, cache_krt, w, l,
                      Tp, DB, DS, nsub)
        ym_p = _mem_prompt(qm, mk_p4, mv_p4, l, B, S, tm)
        ym_s = _mem_sample(qm[Tp:], cmk, cmv, l, DB, DS, nb)
        x = _merge(x, P, (ya_p, yr_p, yc_p, ym_p), (ya_s, yr_s, yc_s, ym_s), w["w_branch"], w["w_out"], l, tm)
        x = _ffn(x, w["norm_ffn2"], w["ffn2_g"], w["ffn2_u"], w["ffn2_d"], l, tm)

        p_lat.append(lat[:Tp])
        s_lat.append(lat[Tp:])
        p_kr.append(kr[:Tp, :ROPE_DIM])
        s_kr.append(kr[Tp:, :ROPE_DIM])
        p_ca.append(ta[:, SUBLANES - ka:])
        p_cb.append(tb[:, SUBLANES - kb:])
        p_h.append(th[:, SUBLANES - 1])
        s_ca.append(va_s.reshape(DB, DS, D_CONV)[:, DS - ka:])
        s_cb.append(P[Tp:, P_CONV + 3 * D_CONV:P_CONV + 3 * D_CONV + D_LRU].reshape(DB, DS, D_LRU)[:, DS - kb:].astype(F32))
        s_h.append(h_s.reshape(DB, DS, D_LRU)[:, DS - 1])

    return (
        x[:Tp].reshape(B, S, D), x[Tp:].reshape(DB, DS, D),
        jnp.stack(p_lat).reshape(depth, B, S, KV_LORA), jnp.stack(p_kr).reshape(depth, B, S, ROPE_DIM),
        jnp.stack(p_ca), jnp.stack(p_cb), jnp.stack(p_h),
        mk_p.reshape(depth, B, N_MEM, MEM_HEADS, MEM_HD), mv_p.reshape(depth, B, N_MEM, MEM_HEADS, MEM_HD),
        jnp.stack(s_lat).reshape(depth, DB, DS, KV_LORA), jnp.stack(s_kr).reshape(depth, DB, DS, ROPE_DIM),
        jnp.stack(s_ca), jnp.stack(s_cb), jnp.stack(s_h),
    )
```

```python
import functools

import numpy as np
import jax
import jax.numpy as jnp
from jax import lax
from jax.experimental import pallas as pl
from jax.experimental.pallas import tpu as pltpu

F32 = jnp.float32
BF16 = jnp.bfloat16

D_FF = 2816
D_CONV = 512
D_LRU = 512
LRU_BLOCKS = 8
LRU_C = 8.0
MLA_HEADS = 8
Q_LORA = 384
KV_LORA = 256
NOPE_DIM = 64
ROPE_DIM = 32
V_DIM = 64
QK_DIM = NOPE_DIM + ROPE_DIM
ROPE_THETA = 10000.0
N_MEM = 256
MEM_HEADS = 4
MEM_HD = 128
MEM_WIDTH = MEM_HEADS * MEM_HD
N_BRANCH = 4
EPS = 1e-6
NEG = -1e30

LANES = 128
SUBLANES = 8
HEAD_SLOT = LANES
VMEM_LIMIT = 56 * 1024 * 1024
LOG2E = 1.4426950408889634
NK = MLA_HEADS * NOPE_DIM
DMA_UNROLL = 8

P_GATES = 0
P_CONV = 4096
P_TAIL = 6144
P_WIDTH = 8192
P_CHUNK = 2048
T_RY, T_CQ, T_CKV, T_KRA, T_KRB, T_MQ = 0, 512, 896, 1152, 1280, 1408


def _dot(a, b):
    return jnp.dot(a, b, preferred_element_type=F32)


def _dot_nt(a, b):
    return lax.dot_general(a, b, (((1,), (1,)), ((), ())), preferred_element_type=F32)


def _rms(x, g):
    return x * lax.rsqrt(jnp.mean(x * x, axis=-1, keepdims=True) + EPS) * g


def _params(*sem):
    return pltpu.CompilerParams(dimension_semantics=sem, vmem_limit_bytes=VMEM_LIMIT)


def _ffn_kernel(x_ref, g_ref, wg_ref, wu_ref, wd_ref, o_ref, xn_ref, acc_ref, *, nf):
    j = pl.program_id(1)

    @pl.when(j == 0)
    def _():
        xn_ref[...] = _rms(x_ref[...], g_ref[...]).astype(BF16)
        acc_ref[...] = jnp.zeros_like(acc_ref)

    xn = xn_ref[...]
    g = _dot(xn, wg_ref[...])
    u = _dot(xn, wu_ref[...])
    h = (g * jax.nn.sigmoid(g) * u).astype(BF16)
    acc_ref[...] += _dot(h, wd_ref[...])

    @pl.when(j == nf - 1)
    def _():
        o_ref[...] = x_ref[...] + 0.5 * acc_ref[...]


def _ffn(x, gain, wg, wu, wd, l, tm):
    T, D = x.shape
    nf = 2
    tf = wg.shape[2] // nf
    return pl.pallas_call(
        functools.partial(_ffn_kernel, nf=nf),
        grid=(T // tm, nf),
        in_specs=[
            pl.BlockSpec((tm, D), lambda i, j: (i, 0)),
            pl.BlockSpec((None, 1, D), lambda i, j: (l, 0, 0)),
            pl.BlockSpec((None, D, tf), lambda i, j: (l, 0, j)),
            pl.BlockSpec((None, D, tf), lambda i, j: (l, 0, j)),
            pl.BlockSpec((None, tf, D), lambda i, j: (l, j, 0)),
        ],
        out_specs=pl.BlockSpec((tm, D), lambda i, j: (i, 0)),
        out_shape=jax.ShapeDtypeStruct((T, D), F32),
        scratch_shapes=[pltpu.VMEM((tm, D), BF16), pltpu.VMEM((tm, D), F32)],
        compiler_params=_params("parallel", "arbitrary"),
        name="ffn",
    )(x, gain, wg, wu, wd)


def _proj_kernel(x_ref, g_ref, w_ref, o_ref, xn_ref, *, n_gate_chunks):
    j = pl.program_id(1)

    @pl.when(j == 0)
    def _():
        xn_ref[...] = _rms(x_ref[...], g_ref[...]).astype(BF16)

    y = _dot(xn_ref[...], w_ref[...])

    @pl.when(j < n_gate_chunks)
    def _():
        o_ref[...] = jax.nn.sigmoid(y).astype(o_ref.dtype)

    @pl.when(j >= n_gate_chunks)
    def _():
        o_ref[...] = y.astype(o_ref.dtype)


def _proj(x, gain, w, l, tm):
    T, D = x.shape
    return pl.pallas_call(
        functools.partial(_proj_kernel, n_gate_chunks=P_CONV // P_CHUNK),
        grid=(T // tm, P_WIDTH // P_CHUNK),
        in_specs=[
            pl.BlockSpec((tm, D), lambda i, j: (i, 0)),
            pl.BlockSpec((None, 1, D), lambda i, j: (l, 0, 0)),
            pl.BlockSpec((None, D, P_CHUNK), lambda i, j: (l, 0, j)),
        ],
        out_specs=pl.BlockSpec((tm, P_CHUNK), lambda i, j: (i, j)),
        out_shape=jax.ShapeDtypeStruct((T, P_WIDTH), BF16),
        scratch_shapes=[pltpu.VMEM((tm, D), BF16)],
        compiler_params=_params("parallel", "arbitrary"),
        name="proj",
    )(x, gain, w)


def _head_norm_store(dst_ref, x, gain, n_heads, inv_width):
    for h in range(n_heads):
        sl = slice(h * HEAD_SLOT, (h + 1) * HEAD_SLOT)
        xh = x[:, sl]
        ms = jnp.sum(xh * xh, axis=-1, keepdims=True) * inv_width
        dst_ref[:, sl] = (xh * lax.rsqrt(ms + EPS) * gain).astype(dst_ref.dtype)


def _prep_kernel(p_ref, cos_ref, sin_ref, qn_ref, wa_ref, wb_ref, qg_ref, kvn_ref, wk_ref, wv_ref,
                 kg_ref, mqg_ref, q_ref, lat_ref, kr_ref, k_ref, v_ref, qm_ref):
    cos = cos_ref[...]
    sin = sin_ref[...]
    n = _rms(p_ref[:, T_CQ:T_CQ + Q_LORA].astype(F32), qn_ref[...]).astype(BF16)
    qa = _dot(n, wa_ref[...])
    qb = _dot(n, wb_ref[...])
    qg = qg_ref[...] * (QK_DIM ** -0.5 * LOG2E)
    for h in range(MLA_HEADS):
        sl = slice(h * HEAD_SLOT, (h + 1) * HEAD_SLOT)
        qh = qa[:, sl] * cos + qb[:, sl] * sin
        ms = jnp.sum(qh * qh, axis=-1, keepdims=True) * (1.0 / QK_DIM)
        q_ref[:, sl] = (qh * lax.rsqrt(ms + EPS) * qg).astype(q_ref.dtype)
    lat = _rms(p_ref[:, T_CKV:T_CKV + KV_LORA].astype(F32), kvn_ref[...])
    lat_ref[...] = lat
    kr = p_ref[:, T_KRA:T_KRA + LANES].astype(F32) * cos + p_ref[:, T_KRB:T_KRB + LANES].astype(F32) * sin
    kr_ref[...] = kr
    latb = lat.astype(BF16)
    kn = _dot(latb, wk_ref[...])
    kg = kg_ref[...]
    for h in range(MLA_HEADS):
        sl = slice(h * HEAD_SLOT, (h + 1) * HEAD_SLOT)
        kh = kn[:, sl] + kr
        ms = jnp.sum(kh * kh, axis=-1, keepdims=True) * (1.0 / QK_DIM)
        k_ref[:, sl] = (kh * lax.rsqrt(ms + EPS) * kg).astype(k_ref.dtype)
    v_ref[...] = _dot(latb, wv_ref[...]).astype(v_ref.dtype)
    _head_norm_store(qm_ref, p_ref[:, T_MQ:T_MQ + MEM_WIDTH].astype(F32), mqg_ref[...] * (MEM_HD ** -0.5),
                     MEM_HEADS, 1.0 / MEM_HD)


def _prep(P, cos, sin, w, l, tm):
    T = P.shape[0]
    HW = MLA_HEADS * HEAD_SLOT
    lay = lambda *shape: pl.BlockSpec((None,) + shape, lambda i: (l,) + (0,) * len(shape))
    row = lambda width: pl.BlockSpec((tm, width), lambda i: (i, 0))
    return pl.pallas_call(
        _prep_kernel,
        grid=(T // tm,),
        in_specs=[
            pl.BlockSpec((tm, P_CHUNK), lambda i: (i, P_TAIL // P_CHUNK)),
            row(LANES), row(LANES),
            lay(1, Q_LORA), lay(Q_LORA, HW), lay(Q_LORA, HW), lay(1, HEAD_SLOT),
            lay(1, KV_LORA), lay(KV_LORA, HW), lay(KV_LORA, MLA_HEADS * V_DIM), lay(1, HEAD_SLOT),
            lay(1, MEM_HD),
        ],
        out_specs=[row(HW), row(KV_LORA), row(LANES), row(HW), row(MLA_HEADS * V_DIM), row(MEM_WIDTH)],
        out_shape=[
            jax.ShapeDtypeStruct((T, HW), BF16),
            jax.ShapeDtypeStruct((T, KV_LORA), F32),
            jax.ShapeDtypeStruct((T, LANES), F32),
            jax.ShapeDtypeStruct((T, HW), BF16),
            jax.ShapeDtypeStruct((T, MLA_HEADS * V_DIM), BF16),
            jax.ShapeDtypeStruct((T, MEM_WIDTH), BF16),
        ],
        compiler_params=_params("parallel"),
        name="prep",
    )(P, cos, sin, w["q_norm"], w["uq_a"], w["uq_b"], w["q_gain"], w["kv_norm"], w["wk_slot"],
      w["wv"], w["k_gain"], w["mq_gain"])


def _lru_coeffs(xr, wr, wi, br, bi, lam):
    xb = xr.astype(BF16)
    r = jax.nn.sigmoid(_dot(xb, wr) + br)
    i = jax.nn.sigmoid(_dot(xb, wi) + bi)
    log_a = -LRU_C * r * jax.nn.softplus(-lam)
    a = jnp.exp(log_a)
    b = jnp.sqrt(1.0 - a * a) * (i * xr)
    return a, b


def _scan(a, b, pos, seg):
    d = 1
    while d < seg:
        keep = pos >= d
        a_s = jnp.where(keep, pltpu.roll(a, d, 0), 1.0)
        b_s = jnp.where(keep, pltpu.roll(b, d, 0), 0.0)
        b = a * b_s + b
        a = a * a_s
        d *= 2
    return a, b


def _conv_taps(x, shifted, w, bias=None):
    K = w.shape[0]
    y = shifted(K - 1) * w[0:1]
    for k in range(1, K - 1):
        y = y + shifted(K - 1 - k) * w[k:k + 1]
    y = y + x * w[K - 1:K]
    if bias is not None:
        y = y + bias
    return y


def _convlru_prompt_kernel(p_ref, ry_ref, caw_ref, cbw_ref, cbb_ref, wr_ref, wi_ref, br_ref, bi_ref,
                           lam_ref, ya_ref, yr_ref, ta_ref, tb_ref, th_ref, pa_ref, pb_ref, h_ref,
                           as_ref, bs_ref, hs_ref):
    tt = p_ref.shape[0]

    @pl.when(pl.program_id(1) == 0)
    def _():
        pa_ref[...] = jnp.zeros_like(pa_ref)
        pb_ref[...] = jnp.zeros_like(pb_ref)
        h_ref[...] = jnp.zeros_like(h_ref)

    row = lax.broadcasted_iota(jnp.int32, (tt, 1), 0)
    row8 = lax.broadcasted_iota(jnp.int32, (SUBLANES, 1), 0)

    def delayed(x, prev8):
        def shifted(s):
            r = pltpu.roll(x, s, 0)
            first = jnp.where(row8 < s, pltpu.roll(prev8, s, 0), r[:SUBLANES])
            return jnp.concatenate([first, r[SUBLANES:]], axis=0)
        return shifted

    a_b = p_ref[:, 0:D_CONV].astype(F32)
    v = p_ref[:, D_CONV:2 * D_CONV].astype(F32) * p_ref[:, 2 * D_CONV:3 * D_CONV].astype(F32)
    u = _conv_taps(v, delayed(v, pa_ref[...]), caw_ref[...])
    ya_ref[...] = (a_b * u).astype(ya_ref.dtype)

    rx = p_ref[:, 3 * D_CONV:3 * D_CONV + D_LRU].astype(F32)
    xr = _conv_taps(rx, delayed(rx, pb_ref[...]), cbw_ref[...], cbb_ref[...])
    a, b = _lru_coeffs(xr, wr_ref[...], wi_ref[...], br_ref[...], bi_ref[...], lam_ref[...])
    as_ref[...], bs_ref[...] = _scan(a, b, row % SUBLANES, SUBLANES)

    def carry_group(g, h_in):
        r = pl.ds(pl.multiple_of(g * SUBLANES, SUBLANES), SUBLANES)
        hg = bs_ref[r, :] + as_ref[r, :] * h_in
        hs_ref[r, :] = hg
        return hg[SUBLANES - 1:SUBLANES, :]

    lax.fori_loop(0, tt // SUBLANES, carry_group, h_ref[...], unroll=8)
    h = hs_ref[...]
    yr_ref[...] = (h * jax.nn.gelu(ry_ref[...].astype(F32))).astype(yr_ref.dtype)

    tail = slice(tt - SUBLANES, tt)
    ta_ref[...] = v[tail]
    tb_ref[...] = rx[tail]
    th_ref[...] = h[tail]
    pa_ref[...] = v[tail]
    pb_ref[...] = rx[tail]
    h_ref[...] = h[tt - 1:tt]


def _convlru_prompt(P, w, l, B, S, tt):
    nt = S // tt
    cb = P_CHUNK // D_CONV
    lay = lambda *shape: pl.BlockSpec((None,) + shape, lambda b, t: (l,) + (0,) * len(shape))
    seq = pl.BlockSpec((tt, D_CONV), lambda b, t: (b * nt + t, 0))
    tail = pl.BlockSpec((None, SUBLANES, D_CONV), lambda b, t: (b, 0, 0))
    tail_shape = jax.ShapeDtypeStruct((B, SUBLANES, D_CONV), F32)
    return pl.pallas_call(
        _convlru_prompt_kernel,
        grid=(B, nt),
        in_specs=[
            pl.BlockSpec((tt, P_CHUNK), lambda b, t: (b * nt + t, P_CONV // P_CHUNK)),
            pl.BlockSpec((tt, D_LRU), lambda b, t: (b * nt + t, P_TAIL // D_LRU)),
            lay(3, D_CONV), lay(4, D_LRU), lay(1, D_LRU), lay(D_LRU, D_LRU), lay(D_LRU, D_LRU),
            lay(1, D_LRU), lay(1, D_LRU), lay(1, D_LRU),
        ],
        out_specs=[seq, seq, tail, tail, tail],
        out_shape=[
            jax.ShapeDtypeStruct((B * S, D_CONV), BF16),
            jax.ShapeDtypeStruct((B * S, D_LRU), BF16),
            tail_shape, tail_shape, tail_shape,
        ],
        scratch_shapes=[pltpu.VMEM((SUBLANES, D_CONV), F32), pltpu.VMEM((SUBLANES, D_LRU), F32),
                        pltpu.VMEM((1, D_LRU), F32)] + [pltpu.VMEM((tt, D_LRU), F32)] * 3,
        compiler_params=_params("parallel", "arbitrary"),
        name="convlru_prompt",
    )(P, P, w["conv_a_w"], w["conv_b_w"], w["conv_b_bias"], w["lru_wr"], w["lru_wi"], w["lru_br"],
      w["lru_bi"], w["lru_lambda"])


def _convlru_sample_kernel(p_ref, ry_ref, sa_ref, sb_ref, h0_ref, caw_ref, cbw_ref, cbb_ref, wr_ref,
                           wi_ref, br_ref, bi_ref, lam_ref, ya_ref, yr_ref, va_ref, h_ref, *, ds):
    ts = p_ref.shape[0]
    pos = lax.broadcasted_iota(jnp.int32, (ts, 1), 0) % ds

    def delayed(x, state):
        def shifted(s):
            return jnp.where(pos < s, pltpu.roll(state, ts - ds + s, 0), pltpu.roll(x, s, 0))
        return shifted

    a_b = p_ref[:, 0:D_CONV].astype(F32)
    v = p_ref[:, D_CONV:2 * D_CONV].astype(F32) * p_ref[:, 2 * D_CONV:3 * D_CONV].astype(F32)
    u = _conv_taps(v, delayed(v, sa_ref[...]), caw_ref[...])
    ya_ref[...] = (a_b * u).astype(ya_ref.dtype)
    va_ref[...] = v

    rx = p_ref[:, 3 * D_CONV:3 * D_CONV + D_LRU].astype(F32)
    xr = _conv_taps(rx, delayed(rx, sb_ref[...]), cbw_ref[...], cbb_ref[...])
    a, b = _lru_coeffs(xr, wr_ref[...], wi_ref[...], br_ref[...], bi_ref[...], lam_ref[...])
    b = b + a * h0_ref[...]
    _, h = _scan(a, b, pos, ds)
    h_ref[...] = h
    yr_ref[...] = (h * jax.nn.gelu(ry_ref[...].astype(F32))).astype(yr_ref.dtype)


def _convlru_sample(P, sa, sb, h0, w, l, row0, Ts, ds, ts):
    assert ds == SUBLANES
    r0 = row0 // ts
    lay = lambda *shape: pl.BlockSpec((None,) + shape, lambda i: (l,) + (0,) * len(shape))
    st = pl.BlockSpec((None, ts, D_CONV), lambda i: (l, i, 0))
    seq = pl.BlockSpec((ts, D_CONV), lambda i: (i, 0))
    return pl.pallas_call(
        functools.partial(_convlru_sample_kernel, ds=ds),
        grid=(Ts // ts,),
        in_specs=[
            pl.BlockSpec((ts, P_CHUNK), lambda i: (r0 + i, P_CONV // P_CHUNK)),
            pl.BlockSpec((ts, D_LRU), lambda i: (r0 + i, P_TAIL // D_LRU)),
            st, st, st,
            lay(3, D_CONV), lay(4, D_LRU), lay(1, D_LRU), lay(D_LRU, D_LRU), lay(D_LRU, D_LRU),
            lay(1, D_LRU), lay(1, D_LRU), lay(1, D_LRU),
        ],
        out_specs=[seq, seq, seq, seq],
        out_shape=[
            jax.ShapeDtypeStruct((Ts, D_CONV), BF16),
            jax.ShapeDtypeStruct((Ts, D_LRU), BF16),
            jax.ShapeDtypeStruct((Ts, D_CONV), F32),
            jax.ShapeDtypeStruct((Ts, D_LRU), F32),
        ],
        compiler_params=_params("parallel"),
        name="convlru_sample",
    )(P, P, sa, sb, h0, w["conv_a_w"], w["conv_b_w"], w["conv_b_bias"], w["lru_wr"], w["lru_wi"],
      w["lru_br"], w["lru_bi"], w["lru_lambda"])


def _flash_kernel(q_ref, k_ref, v_ref, o_ref, *, tq, ch):
    S = k_ref.shape[0]
    qi = pl.program_id(2)
    n_chunks = (qi * tq + tq + ch - 1) // ch
    first_head = lax.broadcasted_iota(jnp.int32, (1, LANES), 1) < V_DIM
    qpos = qi * tq + lax.broadcasted_iota(jnp.int32, (tq, 1), 0)

    def body(n):
        L = n * ch
        v = v_ref[0:L, :]
        one = jnp.ones_like(v)
        vext = (jnp.where(first_head, v, one), jnp.where(first_head, one, v))
        mask = lax.broadcasted_iota(jnp.int32, (1, L), 1) <= qpos
        outs = []
        for hh in range(2):
            sl = slice(hh * HEAD_SLOT, (hh + 1) * HEAD_SLOT)
            s = jnp.where(mask, _dot_nt(q_ref[:, sl], k_ref[0:L, sl]), NEG)
            m = jnp.max(s, axis=-1, keepdims=True)
            a = _dot(jnp.exp2(s - m).astype(BF16), vext[hh])
            outs.append(a / pltpu.roll(a, V_DIM, 1))
        o_ref[...] = jnp.where(first_head, outs[0], outs[1]).astype(o_ref.dtype)

    for n in range(1, S // ch + 1):
        pl.when(n_chunks == n)(functools.partial(body, n))


def _flash(q, k, v, B, S, tq, ch):
    nq = S // tq
    hp = MLA_HEADS // 2
    return pl.pallas_call(
        functools.partial(_flash_kernel, tq=tq, ch=ch),
        grid=(B, hp, nq),
        in_specs=[
            pl.BlockSpec((tq, 2 * HEAD_SLOT), lambda b, h, i: (b * nq + i, h)),
            pl.BlockSpec((S, 2 * HEAD_SLOT), lambda b, h, i: (b, h)),
            pl.BlockSpec((S, 2 * V_DIM), lambda b, h, i: (b, h)),
        ],
        out_specs=pl.BlockSpec((tq, 2 * V_DIM), lambda b, h, i: (b * nq + i, h)),
        out_shape=jax.ShapeDtypeStruct((B * S, MLA_HEADS * V_DIM), BF16),
        compiler_params=_params("parallel", "parallel", "arbitrary"),
        name="flash",
    )(q, k, v)


def _sattn_kernel(pt_ref, q_ref, latn_ref, krn_ref, wkn_ref, wkt_ref, wv_ref, kg_ref, clat_ref, ckrt_ref,
                  o_ref, a_ref, lat_st, kr_st, sem, *, l, n_pages, page, ds, nsub):
    b = pl.program_id(0)
    nb = pl.num_programs(0)
    R = MLA_HEADS * ds
    n = n_pages * page
    sub = n // nsub

    def page_copies(seq, slot, i):
        pid = pt_ref[seq, i]
        dst = pl.ds(pl.multiple_of(i * page, page), page)
        return (pltpu.make_async_copy(clat_ref.at[l, pid], lat_st.at[slot, dst, :], sem.at[0, slot]),
                pltpu.make_async_copy(ckrt_ref.at[l, pid], kr_st.at[slot, :, dst], sem.at[1, slot]))

    def start_all(seq, slot):
        def body(i, c):
            for cp in page_copies(seq, slot, i):
                cp.start()
            return c
        lax.fori_loop(0, n_pages, body, 0, unroll=DMA_UNROLL)

    def wait_all(seq, slot):
        def body(i, c):
            for cp in page_copies(seq, slot, i):
                cp.wait()
            return c
        lax.fori_loop(0, n_pages, body, 0, unroll=DMA_UNROLL)

    slot = b % 2

    @pl.when(b == 0)
    def _():
        start_all(0, 0)

    @pl.when(b + 1 < nb)
    def _():
        start_all(b + 1, 1 - slot)

    qk = q_ref[...] * kg_ref[...]
    qa, qp = [], []
    for h in range(MLA_HEADS):
        qh = qk[:, h * HEAD_SLOT:(h + 1) * HEAD_SLOT]
        qa.append(_dot(qh.astype(BF16), wkt_ref[h]))
        qp.append(qh[:, 0:ROPE_DIM])
    a_ref[0:NK, :] = wkn_ref[...]
    a_ref[NK:NK + R, :] = jnp.concatenate(qa, axis=0).astype(BF16)
    qpe = jnp.concatenate(qp, axis=0).astype(BF16)
    ones_pe = jnp.ones((MLA_HEADS, ROPE_DIM), BF16)

    def update(carry, L, s_pe, ssq_pe, mask):
        m_prev, l_prev, acc = carry
        u = _dot_nt(a_ref[...], L)
        kn = u[0:NK].reshape(MLA_HEADS, NOPE_DIM, -1)
        ssq = jnp.sum(kn * kn, axis=1) + ssq_pe
        rinv = lax.rsqrt(ssq * (1.0 / QK_DIM) + EPS)
        s = ((u[NK:NK + R] + s_pe).reshape(MLA_HEADS, ds, -1) * rinv[:, None, :]).reshape(R, -1)
        if mask is not None:
            s = jnp.where(mask, s, NEG)
        m_new = jnp.maximum(m_prev, jnp.max(s, axis=-1, keepdims=True))
        al = jnp.exp2(m_prev - m_new)
        pr = jnp.exp2(s - m_new)
        l_new = al * l_prev + jnp.sum(pr, axis=-1, keepdims=True)
        return m_new, l_new, al * acc + _dot(pr.astype(BF16), L)

    Ln = jnp.concatenate([latn_ref[...], jnp.zeros((page - ds, KV_LORA), F32)], axis=0).astype(BF16)
    Rn = jnp.concatenate([krn_ref[:, 0:ROPE_DIM], jnp.zeros((page - ds, ROPE_DIM), F32)], axis=0)
    key = lax.broadcasted_iota(jnp.int32, (R, page), 1)
    qry = lax.broadcasted_iota(jnp.int32, (R, page), 0) % ds
    carry = (jnp.full((R, 1), NEG, F32), jnp.zeros((R, 1), F32), jnp.zeros((R, KV_LORA), F32))
    carry = update(carry, Ln, _dot_nt(qpe, Rn.astype(BF16)), _dot_nt(ones_pe, (Rn * Rn).astype(BF16)),
                   key <= qry)

    wait_all(b, slot)
    for c in range(nsub):
        L = lat_st[slot, c * sub:(c + 1) * sub, :].astype(BF16)
        rt = kr_st[slot, :, c * sub:(c + 1) * sub]
        rf = rt.astype(F32)
        carry = update(carry, L, _dot(qpe, rt), _dot(ones_pe, (rf * rf).astype(BF16)), None)

    _, l_fin, acc = carry
    o_lat = (acc / l_fin).astype(BF16)
    y = _dot(o_lat, wv_ref[...])
    vrow = lax.broadcasted_iota(jnp.int32, (R, MLA_HEADS * V_DIM), 0) // ds
    vcol = lax.broadcasted_iota(jnp.int32, (R, MLA_HEADS * V_DIM), 1) // V_DIM
    y = jnp.where(vrow == vcol, y, 0.0)
    o_ref[...] = jnp.sum(y.reshape(MLA_HEADS, ds, MLA_HEADS * V_DIM), axis=0)


def _sattn(page_table, q_s, lat, kr, cache_lat, cache_krt, w, l, row0, DB, ds, nsub):
    n_pages = page_table.shape[1]
    page = cache_lat.shape[2]
    assert ds == SUBLANES and MLA_HEADS == SUBLANES and (n_pages * page) % nsub == 0
    HW = MLA_HEADS * HEAD_SLOT
    R = MLA_HEADS * ds
    r0 = row0 // ds
    lay = lambda *shape: pl.BlockSpec((None,) + shape, lambda b, pt: (l,) + (0,) * len(shape))
    grid_spec = pltpu.PrefetchScalarGridSpec(
        num_scalar_prefetch=1,
        grid=(DB,),
        in_specs=[
            pl.BlockSpec((ds, HW), lambda b, pt: (b, 0)),
            pl.BlockSpec((ds, KV_LORA), lambda b, pt: (r0 + b, 0)),
            pl.BlockSpec((ds, LANES), lambda b, pt: (r0 + b, 0)),
            lay(NK, KV_LORA), lay(MLA_HEADS, HEAD_SLOT, KV_LORA), lay(KV_LORA, MLA_HEADS * V_DIM), lay(1, HW),
            pl.BlockSpec(memory_space=pl.ANY), pl.BlockSpec(memory_space=pl.ANY),
        ],
        out_specs=pl.BlockSpec((ds, MLA_HEADS * V_DIM), lambda b, pt: (b, 0)),
        scratch_shapes=[
            pltpu.VMEM((NK + R, KV_LORA), BF16),
            pltpu.VMEM((2, n_pages * page, KV_LORA), F32),
            pltpu.VMEM((2, ROPE_DIM, n_pages * page), BF16),
            pltpu.SemaphoreType.DMA((2, 2)),
        ],
    )
    return pl.pallas_call(
        functools.partial(_sattn_kernel, l=l, n_pages=n_pages, page=page, ds=ds, nsub=nsub),
        grid_spec=grid_spec,
        out_shape=jax.ShapeDtypeStruct((DB * ds, MLA_HEADS * V_DIM), F32),
        compiler_params=_params("arbitrary"),
        name="sattn",
    )(page_table, q_s, lat, kr, w["wk_nt"], w["wk_t"], w["wv"], w["k_gain_tiled"], cache_lat, cache_krt)


def _softmax_rows(s):
    m = jnp.max(s, axis=-1, keepdims=True)
    p = jnp.exp(s - m)
    return p / jnp.sum(p, axis=-1, keepdims=True)


def _mem_prompt_kernel(q_ref, k_ref, v_ref, o_ref):
    for h in range(MEM_HEADS):
        sl = slice(h * MEM_HD, (h + 1) * MEM_HD)
        p = _softmax_rows(_dot_nt(q_ref[:, sl], k_ref[:, sl].astype(BF16)))
        o_ref[:, sl] = _dot(p.astype(BF16), v_ref[:, sl].astype(BF16)).astype(o_ref.dtype)


def _mem_prompt(qm, mk, mv, l, B, S, tm):
    nt = S // tm
    kv = pl.BlockSpec((None, None, N_MEM, MEM_WIDTH), lambda b, t: (l, b, 0, 0))
    seq = pl.BlockSpec((tm, MEM_WIDTH), lambda b, t: (b * nt + t, 0))
    return pl.pallas_call(
        _mem_prompt_kernel,
        grid=(B, nt),
        in_specs=[seq, kv, kv],
        out_specs=seq,
        out_shape=jax.ShapeDtypeStruct((B * S, MEM_WIDTH), BF16),
        compiler_params=_params("parallel", "parallel"),
        name="mem_prompt",
    )(qm, mk, mv)


def _mem_sample_kernel(q_ref, k_ref, v_ref, o_ref, *, nb, ds):
    q = q_ref[...].reshape(nb, ds, MEM_WIDTH)
    outs = []
    for h in range(MEM_HEADS):
        sl = slice(h * MEM_HD, (h + 1) * MEM_HD)
        s = jnp.einsum('bqd,bkd->bqk', q[:, :, sl], k_ref[:, :, sl].astype(BF16),
                       preferred_element_type=F32)
        p = _softmax_rows(s)
        outs.append(jnp.einsum('bqk,bkd->bqd', p.astype(BF16), v_ref[:, :, sl].astype(BF16),
                               preferred_element_type=F32))
    o_ref[...] = jnp.concatenate(outs, axis=-1).reshape(nb * ds, MEM_WIDTH)


def _mem_sample(qm_s, ck, cv, l, DB, ds, nb):
    kv = pl.BlockSpec((None, nb, N_MEM, MEM_WIDTH), lambda i: (l, i, 0, 0))
    seq = pl.BlockSpec((nb * ds, MEM_WIDTH), lambda i: (i, 0))
    return pl.pallas_call(
        functools.partial(_mem_sample_kernel, nb=nb, ds=ds),
        grid=(DB // nb,),
        in_specs=[seq, kv, kv],
        out_specs=seq,
        out_shape=jax.ShapeDtypeStruct((DB * ds, MEM_WIDTH), F32),
        compiler_params=_params("parallel"),
        name="mem_sample",
    )(qm_s, ck, cv)


def _memkv_kernel(x_ref, n_ref, w_ref, kg_ref, k_ref, v_ref):
    kv = _dot(_rms(x_ref[...], n_ref[...]).astype(BF16), w_ref[...])
    _head_norm_store(k_ref, kv[:, :MEM_WIDTH], kg_ref[...], MEM_HEADS, 1.0 / MEM_HD)
    v_ref[...] = kv[:, MEM_WIDTH:]


def _memkv(mem, mem_norm, w_kv, k_gain):
    depth = w_kv.shape[0]
    R, D = mem.shape
    lay = lambda *shape: pl.BlockSpec((None,) + shape, lambda l: (l,) + (0,) * len(shape))
    out = jax.ShapeDtypeStruct((depth, R, MEM_WIDTH), F32)
    return pl.pallas_call(
        _memkv_kernel,
        grid=(depth,),
        in_specs=[pl.BlockSpec((R, D), lambda l: (0, 0)), lay(1, D), lay(D, 2 * MEM_WIDTH), lay(1, MEM_HD)],
        out_specs=[lay(R, MEM_WIDTH), lay(R, MEM_WIDTH)],
        out_shape=[out, out],
        compiler_params=_params("parallel"),
        name="memkv",
    )(mem, mem_norm, w_kv, k_gain)


def _merge_kernel(x_ref, sg_ref, yap_ref, yrp_ref, ycp_ref, ymp_ref, yas_ref, yrs_ref, ycs_ref, yms_ref,
                  wb_ref, wo_ref, o_ref, *, n_prompt_tiles):
    D = x_ref.shape[1]

    def run(y_refs):
        merged = None
        for bi, y_ref in enumerate(y_refs):
            width = y_ref.shape[1]
            t = sg_ref[:, bi * D:(bi + 1) * D].astype(F32) * _dot(
                y_ref[...].astype(BF16), wb_ref[bi * width:(bi + 1) * width, :])
            merged = t if merged is None else merged + t
        o_ref[...] = x_ref[...] + _dot(merged.astype(BF16), wo_ref[...])

    is_prompt = pl.program_id(0) < n_prompt_tiles
    pl.when(is_prompt)(lambda: run((yap_ref, yrp_ref, ycp_ref, ymp_ref)))
    pl.when(jnp.logical_not(is_prompt))(lambda: run((yas_ref, yrs_ref, ycs_ref, yms_ref)))


def _merge(x, P, y_prompt, y_sample, wb, wo, l, tm):
    T, D = x.shape
    npt = y_prompt[0].shape[0] // tm
    row = lambda width: pl.BlockSpec((tm, width), lambda i: (i, 0))
    prow = lambda y: pl.BlockSpec((tm, y.shape[1]), lambda i: (jnp.minimum(i, npt - 1), 0))
    srow = lambda y: pl.BlockSpec((tm, y.shape[1]), lambda i: (jnp.maximum(i - npt, 0), 0))
    lay = lambda *shape: pl.BlockSpec((None,) + shape, lambda i: (l,) + (0,) * len(shape))
    return pl.pallas_call(
        functools.partial(_merge_kernel, n_prompt_tiles=npt),
        grid=(T // tm,),
        in_specs=[row(D), row(N_BRANCH * D)] + [prow(y) for y in y_prompt] + [srow(y) for y in y_sample]
                 + [lay(wb.shape[1], D), lay(D, D)],
        out_specs=row(D),
        out_shape=jax.ShapeDtypeStruct((T, D), F32),
        compiler_params=_params("parallel"),
        name="merge",
    )(x, P, *y_prompt, *y_sample, wb, wo)


def _prepare_weights(a):
    depth, D = a["norm_ffn1"].shape
    bf = lambda t: t.astype(BF16)
    vec = lambda t: t.reshape(depth, 1, -1)
    w = {}
    for name in ("ffn1", "ffn2"):
        w_in = a[name + "_w_in"]
        w[name + "_g"] = bf(w_in[:, :, :D_FF])
        w[name + "_u"] = bf(w_in[:, :, D_FF:])
        w[name + "_d"] = bf(a[name + "_w_out"])
    w["norm_ffn1"], w["norm_ffn2"], w["norm_mix"] = vec(a["norm_ffn1"]), vec(a["norm_ffn2"]), vec(a["norm_mix"])

    splits = (D_CONV, D_CONV, D_CONV, D_LRU, D_LRU, Q_LORA, KV_LORA, ROPE_DIM, MEM_WIDTH) + (D,) * N_BRANCH
    off = np.concatenate([[0], np.cumsum(splits)])
    w_in = a["w_in"]
    piece = lambda i: w_in[:, :, off[i]:off[i + 1]]
    zeros = lambda n: jnp.zeros((depth, D, n), w_in.dtype)
    half = ROPE_DIM // 2
    c_kr = piece(7)
    kr_a = jnp.concatenate([c_kr, zeros(LANES - ROPE_DIM)], axis=-1)
    kr_b = jnp.concatenate([-c_kr[..., half:], c_kr[..., :half], zeros(LANES - ROPE_DIM)], axis=-1)
    w["w_in"] = bf(jnp.concatenate(
        [w_in[:, :, off[9]:]] + [piece(i) for i in range(7)] + [kr_a, kr_b, piece(8), zeros(LANES)], axis=-1))
    assert w["w_in"].shape[-1] == P_WIDTH

    def slot_gain(g):
        return jnp.concatenate([g[:, NOPE_DIM:], g[:, :NOPE_DIM],
                                jnp.zeros((depth, HEAD_SLOT - QK_DIM), g.dtype)], axis=-1)[:, None, :]

    uq = a["mla_w_uq"].reshape(depth, Q_LORA, MLA_HEADS, QK_DIM)
    nope, pe = uq[..., :NOPE_DIM], uq[..., NOPE_DIM:]
    z = lambda n: jnp.zeros((depth, Q_LORA, MLA_HEADS, n), uq.dtype)
    w["uq_a"] = bf(jnp.concatenate([pe, nope, z(HEAD_SLOT - QK_DIM)], axis=-1).reshape(depth, Q_LORA, -1))
    w["uq_b"] = bf(jnp.concatenate([-pe[..., half:], pe[..., :half], z(HEAD_SLOT - ROPE_DIM)],
                                   axis=-1).reshape(depth, Q_LORA, -1))
    w["q_norm"], w["kv_norm"] = vec(a["mla_q_norm"]), vec(a["mla_kv_norm"])
    w["q_gain"] = slot_gain(a["mla_q_gain"])
    w["k_gain"] = slot_gain(a["mla_k_gain"])
    w["k_gain_tiled"] = jnp.tile(w["k_gain"], (1, 1, MLA_HEADS))
    w["mq_gain"] = vec(a["mem_q_gain"])

    ukv = a["mla_w_ukv"].reshape(depth, KV_LORA, MLA_HEADS, NOPE_DIM + V_DIM)
    k_nope, vv = ukv[..., :NOPE_DIM], ukv[..., NOPE_DIM:]
    zk = lambda n: jnp.zeros((depth, KV_LORA, MLA_HEADS, n), ukv.dtype)
    wk_slot = jnp.concatenate([zk(ROPE_DIM), k_nope, zk(HEAD_SLOT - QK_DIM)], axis=-1)
    w["wk_slot"] = bf(wk_slot.reshape(depth, KV_LORA, -1))
    w["wk_t"] = bf(wk_slot.transpose(0, 2, 3, 1))
    w["wk_nt"] = bf(k_nope.reshape(depth, KV_LORA, -1).transpose(0, 2, 1))
    w["wv"] = bf(vv.reshape(depth, KV_LORA, -1))

    gw = a["lru_gate_w"]
    blk = D_LRU // LRU_BLOCKS
    eye = jnp.eye(LRU_BLOCKS, dtype=gw.dtype)[None, :, None, :, None]
    dense = lambda part: (eye * part[:, :, :, None, :]).reshape(depth, D_LRU, D_LRU)
    w["lru_wr"], w["lru_wi"] = bf(dense(gw[..., :blk])), bf(dense(gw[..., blk:]))
    gb = a["lru_gate_b"]
    w["lru_br"] = gb[..., :blk].reshape(depth, 1, D_LRU)
    w["lru_bi"] = gb[..., blk:].reshape(depth, 1, D_LRU)
    w["lru_lambda"], w["conv_b_bias"] = vec(a["lru_lambda"]), vec(a["conv_b_bias"])
    w["conv_a_w"], w["conv_b_w"] = a["conv_a_w"], a["conv_b_w"]

    w["mem_norm"], w["mem_k_gain"] = vec(a["mem_norm"]), vec(a["mem_k_gain"])
    w["mem_w_kv"] = bf(a["mem_w_kv"])
    w["w_branch"], w["w_out"] = bf(a["w_branch"]), bf(a["w_out"])
    return w


def _rope_tables(pos):
    half = ROPE_DIM // 2
    inv = ROPE_THETA ** (-jnp.arange(half, dtype=F32) / half)
    ang = pos.astype(F32)[:, None] * inv[None, :]
    cos, sin = jnp.cos(ang), jnp.sin(ang)
    n = pos.shape[0]
    cos_t = jnp.concatenate([cos, cos, jnp.ones((n, NOPE_DIM), F32), jnp.zeros((n, HEAD_SLOT - QK_DIM), F32)], 1)
    sin_t = jnp.concatenate([sin, sin, jnp.zeros((n, HEAD_SLOT - ROPE_DIM), F32)], axis=1)
    return cos_t, sin_t


def _tile_sizes(T, S, Ts):
    pick = lambda n, opts: next(o for o in opts if n % o == 0)
    return dict(
        tm=pick(np.gcd(T, S), (512, 256, 128)),
        tt=pick(S, (512, 256, 128)),
        ts=pick(Ts, (256, 128, 64, 32, 16, 8)),
        tq=pick(S, (256, 128)),
    )


def kernel(x_prompt, x_sample, cache_kv_latent, cache_k_rope, cache_mem_k, cache_mem_v, state_conv_a, state_conv_b, state_lru_h, page_table, mem_prompt, norm_ffn1, ffn1_w_in, ffn1_w_out, norm_mix, w_in, conv_a_w, conv_b_w, conv_b_bias, lru_gate_w, lru_gate_b, lru_lambda, mla_q_norm, mla_w_uq, mla_kv_norm, mla_w_ukv, mla_q_gain, mla_k_gain, mem_norm, mem_w_kv, mem_k_gain, mem_q_gain, w_branch, w_out, norm_ffn2, ffn2_w_in, ffn2_w_out):
    B, S, D = x_prompt.shape
    DB, DS, _ = x_sample.shape
    depth = norm_ffn1.shape[0]
    page = cache_kv_latent.shape[2]
    past = page_table.shape[1] * page
    Tp, Ts = B * S, DB * DS
    T = Tp + Ts
    ts_ = _tile_sizes(T, S, Ts)
    tm, tt, ts, tq = ts_["tm"], ts_["tt"], ts_["ts"], ts_["tq"]
    assert Tp % ts == 0 and Tp % DS == 0

    w = _prepare_weights(dict(
        norm_ffn1=norm_ffn1, ffn1_w_in=ffn1_w_in, ffn1_w_out=ffn1_w_out, norm_mix=norm_mix, w_in=w_in,
        conv_a_w=conv_a_w, conv_b_w=conv_b_w, conv_b_bias=conv_b_bias, lru_gate_w=lru_gate_w,
        lru_gate_b=lru_gate_b, lru_lambda=lru_lambda, mla_q_norm=mla_q_norm, mla_w_uq=mla_w_uq,
        mla_kv_norm=mla_kv_norm, mla_w_ukv=mla_w_ukv, mla_q_gain=mla_q_gain, mla_k_gain=mla_k_gain,
        mem_norm=mem_norm, mem_w_kv=mem_w_kv, mem_k_gain=mem_k_gain, mem_q_gain=mem_q_gain,
        w_branch=w_branch, w_out=w_out, norm_ffn2=norm_ffn2, ffn2_w_in=ffn2_w_in, ffn2_w_out=ffn2_w_out))

    pos = jnp.concatenate([jnp.tile(jnp.arange(S), B), jnp.tile(past + jnp.arange(DS), DB)])
    cos_t, sin_t = _rope_tables(pos)

    ka, kb = state_conv_a.shape[2], state_conv_b.shape[2]
    sa = jnp.pad(state_conv_a, ((0, 0), (0, 0), (DS - ka, 0), (0, 0))).reshape(depth, Ts, D_CONV)
    sb = jnp.pad(state_conv_b, ((0, 0), (0, 0), (DS - kb, 0), (0, 0))).reshape(depth, Ts, D_LRU)
    h0 = jnp.pad(state_lru_h[:, :, None, :], ((0, 0), (0, 0), (0, DS - 1), (0, 0))).reshape(depth, Ts, D_LRU)
    cmk = cache_mem_k.reshape(depth, DB, N_MEM, MEM_WIDTH)
    cmv = cache_mem_v.reshape(depth, DB, N_MEM, MEM_WIDTH)
    cache_krt = jnp.swapaxes(cache_k_rope, 2, 3).astype(BF16)

    mk_p, mv_p = _memkv(mem_prompt.reshape(B * N_MEM, D), w["mem_norm"], w["mem_w_kv"], w["mem_k_gain"])
    mk_p4 = mk_p.reshape(depth, B, N_MEM, MEM_WIDTH)
    mv_p4 = mv_p.reshape(depth, B, N_MEM, MEM_WIDTH)

    x = jnp.concatenate([x_prompt.reshape(Tp, D), x_sample.reshape(Ts, D)], axis=0)
    p_lat, p_kr, p_ca, p_cb, p_h, s_lat, s_kr, s_ca, s_cb, s_h = ([] for _ in range(10))
    n_pages = page_table.shape[1]
    nsub = next(n for n in (4, 2, 1) if n_pages % n == 0)
    nb = next(n for n in (8, 4, 2, 1) if DB % n == 0)
    ch = max(S // 8, tq)
    for l in range(depth):
        x = _ffn(x, w["norm_ffn1"], w["ffn1_g"], w["ffn1_u"], w["ffn1_d"], l, tm)
        P = _proj(x, w["norm_mix"], w["w_in"], l, tm)
        q, lat, kr, k, v, qm = _prep(P, cos_t, sin_t, w, l, tm)
        ya_p, yr_p, ta, tb, th = _convlru_prompt(P, w, l, B, S, tt)
        ya_s, yr_s, va_s, h_s = _convlru_sample(P, sa, sb, h0, w, l, Tp, Ts, DS, ts)
        yc_p = _flash(q, k, v, B, S, tq, ch)
        yc_s = _sattn(page_table, q[Tp:].astype(F32), lat, kr, cache_kv_latent, cache_krt, w, l,
                      Tp, DB, DS, nsub)
        ym_p = _mem_prompt(qm, mk_p4, mv_p4, l, B, S, tm)
        ym_s = _mem_sample(qm[Tp:], cmk, cmv, l, DB, DS, nb)
        x = _merge(x, P, (ya_p, yr_p, yc_p, ym_p), (ya_s, yr_s, yc_s, ym_s), w["w_branch"], w["w_out"], l, tm)
        x = _ffn(x, w["norm_ffn2"], w["ffn2_g"], w["ffn2_u"], w["ffn2_d"], l, tm)

        p_lat.append(lat[:Tp])
        s_lat.append(lat[Tp:])
        p_kr.append(kr[:Tp, :ROPE_DIM])
        s_kr.append(kr[Tp:, :ROPE_DIM])
        p_ca.append(ta[:, SUBLANES - ka:])
        p_cb.append(tb[:, SUBLANES - kb:])
        p_h.append(th[:, SUBLANES - 1])
        s_ca.append(va_s.reshape(DB, DS, D_CONV)[:, DS - ka:])
        s_cb.append(P[Tp:, P_CONV + 3 * D_CONV:P_CONV + 3 * D_CONV + D_LRU].reshape(DB, DS, D_LRU)[:, DS - kb:].astype(F32))
        s_h.append(h_s.reshape(DB, DS, D_LRU)[:, DS - 1])

    return (
        x[:Tp].reshape(B, S, D), x[Tp:].reshape(DB, DS, D),
        jnp.stack(p_lat).reshape(depth, B, S, KV_LORA), jnp.stack(p_kr).reshape(depth, B, S, ROPE_DIM),
        jnp.stack(p_ca), jnp.stack(p_cb), jnp.stack(p_h),
        mk_p.reshape(depth, B, N_MEM, MEM_HEADS, MEM_HD), mv_p.reshape(depth, B, N_MEM, MEM_HEADS, MEM_HD),
        jnp.stack(s_lat).reshape(depth, DB, DS, KV_LORA), jnp.stack(s_kr).reshape(depth, DB, DS, ROPE_DIM),
        jnp.stack(s_ca), jnp.stack(s_cb), jnp.stack(s_h),
    )
```

```python
import functools

import numpy as np
import jax
import jax.numpy as jnp
from jax import lax
from jax.experimental import pallas as pl
from jax.experimental.pallas import tpu as pltpu

F32 = jnp.float32
BF16 = jnp.bfloat16

D_FF = 2816
D_CONV = 512
D_LRU = 512
LRU_BLOCKS = 8
LRU_C = 8.0
MLA_HEADS = 8
Q_LORA = 384
KV_LORA = 256
NOPE_DIM = 64
ROPE_DIM = 32
V_DIM = 64
QK_DIM = NOPE_DIM + ROPE_DIM
ROPE_THETA = 10000.0
N_MEM = 256
MEM_HEADS = 4
MEM_HD = 128
MEM_WIDTH = MEM_HEADS * MEM_HD
N_BRANCH = 4
EPS = 1e-6
NEG = -1e30

LANES = 128
SUBLANES = 8
HEAD_SLOT = LANES
VMEM_LIMIT = 56 * 1024 * 1024
LOG2E = 1.4426950408889634
NK = MLA_HEADS * NOPE_DIM
DMA_UNROLL = 8

P_GATES = 0
P_CONV = 4096
P_TAIL = 6144
P_WIDTH = 8192
P_CHUNK = 2048
T_RY, T_CQ, T_CKV, T_KRA, T_KRB, T_MQ = 0, 512, 896, 1152, 1280, 1408


def _dot(a, b):
    return jnp.dot(a, b, preferred_element_type=F32)


def _dot_nt(a, b):
    return lax.dot_general(a, b, (((1,), (1,)), ((), ())), preferred_element_type=F32)


def _rms(x, g):
    return x * lax.rsqrt(jnp.mean(x * x, axis=-1, keepdims=True) + EPS) * g


def _params(*sem):
    return pltpu.CompilerParams(dimension_semantics=sem, vmem_limit_bytes=VMEM_LIMIT)


def _ffn_kernel(x_ref, g_ref, wg_ref, wu_ref, wd_ref, o_ref, xn_ref, acc_ref, *, nf):
    j = pl.program_id(1)

    @pl.when(j == 0)
    def _():
        xn_ref[...] = _rms(x_ref[...], g_ref[...]).astype(BF16)
        acc_ref[...] = jnp.zeros_like(acc_ref)

    xn = xn_ref[...]
    g = _dot(xn, wg_ref[...])
    u = _dot(xn, wu_ref[...])
    h = (g * jax.nn.sigmoid(g) * u).astype(BF16)
    acc_ref[...] += _dot(h, wd_ref[...])

    @pl.when(j == nf - 1)
    def _():
        o_ref[...] = x_ref[...] + 0.5 * acc_ref[...]


def _ffn(x, gain, wg, wu, wd, l, tm):
    T, D = x.shape
    nf = 2
    tf = wg.shape[2] // nf
    return pl.pallas_call(
        functools.partial(_ffn_kernel, nf=nf),
        grid=(T // tm, nf),
        in_specs=[
            pl.BlockSpec((tm, D), lambda i, j: (i, 0)),
            pl.BlockSpec((None, 1, D), lambda i, j: (l, 0, 0)),
            pl.BlockSpec((None, D, tf), lambda i, j: (l, 0, j)),
            pl.BlockSpec((None, D, tf), lambda i, j: (l, 0, j)),
            pl.BlockSpec((None, tf, D), lambda i, j: (l, j, 0)),
        ],
        out_specs=pl.BlockSpec((tm, D), lambda i, j: (i, 0)),
        out_shape=jax.ShapeDtypeStruct((T, D), F32),
        scratch_shapes=[pltpu.VMEM((tm, D), BF16), pltpu.VMEM((tm, D), F32)],
        compiler_params=_params("parallel", "arbitrary"),
        name="ffn",
    )(x, gain, wg, wu, wd)


def _proj_kernel(x_ref, g_ref, w_ref, o_ref, xn_ref, *, n_gate_chunks):
    j = pl.program_id(1)

    @pl.when(j == 0)
    def _():
        xn_ref[...] = _rms(x_ref[...], g_ref[...]).astype(BF16)

    y = _dot(xn_ref[...], w_ref[...])

    @pl.when(j < n_gate_chunks)
    def _():
        o_ref[...] = jax.nn.sigmoid(y).astype(o_ref.dtype)

    @pl.when(j >= n_gate_chunks)
    def _():
        o_ref[...] = y.astype(o_ref.dtype)


def _proj(x, gain, w, l, tm):
    T, D = x.shape
    return pl.pallas_call(
        functools.partial(_proj_kernel, n_gate_chunks=P_CONV // P_CHUNK),
        grid=(T // tm, P_WIDTH // P_CHUNK),
        in_specs=[
            pl.BlockSpec((tm, D), lambda i, j: (i, 0)),
            pl.BlockSpec((None, 1, D), lambda i, j: (l, 0, 0)),
            pl.BlockSpec((None, D, P_CHUNK), lambda i, j: (l, 0, j)),
        ],
        out_specs=pl.BlockSpec((tm, P_CHUNK), lambda i, j: (i, j)),
        out_shape=jax.ShapeDtypeStruct((T, P_WIDTH), BF16),
        scratch_shapes=[pltpu.VMEM((tm, D), BF16)],
        compiler_params=_params("parallel", "arbitrary"),
        name="proj",
    )(x, gain, w)


def _head_norm_store(dst_ref, x, gain, n_heads, inv_width):
    for h in range(n_heads):
        sl = slice(h * HEAD_SLOT, (h + 1) * HEAD_SLOT)
        xh = x[:, sl]
        ms = jnp.sum(xh * xh, axis=-1, keepdims=True) * inv_width
        dst_ref[:, sl] = (xh * lax.rsqrt(ms + EPS) * gain).astype(dst_ref.dtype)


def _prep_kernel(p_ref, cos_ref, sin_ref, qn_ref, wa_ref, wb_ref, qg_ref, kvn_ref, wk_ref, wv_ref,
                 kg_ref, mqg_ref, q_ref, lat_ref, kr_ref, k_ref, v_ref, qm_ref):
    cos = cos_ref[...]
    sin = sin_ref[...]
    n = _rms(p_ref[:, T_CQ:T_CQ + Q_LORA].astype(F32), qn_ref[...]).astype(BF16)
    qa = _dot(n, wa_ref[...])
    qb = _dot(n, wb_ref[...])
    qg = qg_ref[...] * (QK_DIM ** -0.5 * LOG2E)
    for h in range(MLA_HEADS):
        sl = slice(h * HEAD_SLOT, (h + 1) * HEAD_SLOT)
        qh = qa[:, sl] * cos + qb[:, sl] * sin
        ms = jnp.sum(qh * qh, axis=-1, keepdims=True) * (1.0 / QK_DIM)
        q_ref[:, sl] = (qh * lax.rsqrt(ms + EPS) * qg).astype(q_ref.dtype)
    lat = _rms(p_ref[:, T_CKV:T_CKV + KV_LORA].astype(F32), kvn_ref[...])
    lat_ref[...] = lat
    kr = p_ref[:, T_KRA:T_KRA + LANES].astype(F32) * cos + p_ref[:, T_KRB:T_KRB + LANES].astype(F32) * sin
    kr_ref[...] = kr
    latb = lat.astype(BF16)
    kn = _dot(latb, wk_ref[...])
    kg = kg_ref[...]
    for h in range(MLA_HEADS):
        sl = slice(h * HEAD_SLOT, (h + 1) * HEAD_SLOT)
        kh = kn[:, sl] + kr
        ms = jnp.sum(kh * kh, axis=-1, keepdims=True) * (1.0 / QK_DIM)
        k_ref[:, sl] = (kh * lax.rsqrt(ms + EPS) * kg).astype(k_ref.dtype)
    v_ref[...] = _dot(latb, wv_ref[...]).astype(v_ref.dtype)
    _head_norm_store(qm_ref, p_ref[:, T_MQ:T_MQ + MEM_WIDTH].astype(F32), mqg_ref[...] * (MEM_HD ** -0.5),
                     MEM_HEADS, 1.0 / MEM_HD)


def _prep(P, cos, sin, w, l, tm):
    T = P.shape[0]
    HW = MLA_HEADS * HEAD_SLOT
    lay = lambda *shape: pl.BlockSpec((None,) + shape, lambda i: (l,) + (0,) * len(shape))
    row = lambda width: pl.BlockSpec((tm, width), lambda i: (i, 0))
    return pl.pallas_call(
        _prep_kernel,
        grid=(T // tm,),
        in_specs=[
            pl.BlockSpec((tm, P_CHUNK), lambda i: (i, P_TAIL // P_CHUNK)),
            row(LANES), row(LANES),
            lay(1, Q_LORA), lay(Q_LORA, HW), lay(Q_LORA, HW), lay(1, HEAD_SLOT),
            lay(1, KV_LORA), lay(KV_LORA, HW), lay(KV_LORA, MLA_HEADS * V_DIM), lay(1, HEAD_SLOT),
            lay(1, MEM_HD),
        ],
        out_specs=[row(HW), row(KV_LORA), row(LANES), row(HW), row(MLA_HEADS * V_DIM), row(MEM_WIDTH)],
        out_shape=[
            jax.ShapeDtypeStruct((T, HW), BF16),
            jax.ShapeDtypeStruct((T, KV_LORA), F32),
            jax.ShapeDtypeStruct((T, LANES), F32),
            jax.ShapeDtypeStruct((T, HW), BF16),
            jax.ShapeDtypeStruct((T, MLA_HEADS * V_DIM), BF16),
            jax.ShapeDtypeStruct((T, MEM_WIDTH), BF16),
        ],
        compiler_params=_params("parallel"),
        name="prep",
    )(P, cos, sin, w["q_norm"], w["uq_a"], w["uq_b"], w["q_gain"], w["kv_norm"], w["wk_slot"],
      w["wv"], w["k_gain"], w["mq_gain"])


def _lru_coeffs(xr, wr, wi, br, bi, lam):
    xb = xr.astype(BF16)
    r = jax.nn.sigmoid(_dot(xb, wr) + br)
    i = jax.nn.sigmoid(_dot(xb, wi) + bi)
    log_a = -LRU_C * r * jax.nn.softplus(-lam)
    a = jnp.exp(log_a)
    b = jnp.sqrt(1.0 - a * a) * (i * xr)
    return a, b


def _scan(a, b, pos, seg):
    d = 1
    while d < seg:
        keep = pos >= d
        a_s = jnp.where(keep, pltpu.roll(a, d, 0), 1.0)
        b_s = jnp.where(keep, pltpu.roll(b, d, 0), 0.0)
        b = a * b_s + b
        a = a * a_s
        d *= 2
    return a, b


def _conv_taps(x, shifted, w, bias=None):
    K = w.shape[0]
    y = shifted(K - 1) * w[0:1]
    for k in range(1, K - 1):
        y = y + shifted(K - 1 - k) * w[k:k + 1]
    y = y + x * w[K - 1:K]
    if bias is not None:
        y = y + bias
    return y


def _convlru_prompt_kernel(p_ref, ry_ref, caw_ref, cbw_ref, cbb_ref, wr_ref, wi_ref, br_ref, bi_ref,
                           lam_ref, ya_ref, yr_ref, ta_ref, tb_ref, th_ref, pa_ref, pb_ref, h_ref,
                           as_ref, bs_ref, hs_ref):
    tt = p_ref.shape[0]

    @pl.when(pl.program_id(1) == 0)
    def _():
        pa_ref[...] = jnp.zeros_like(pa_ref)
        pb_ref[...] = jnp.zeros_like(pb_ref)
        h_ref[...] = jnp.zeros_like(h_ref)

    row = lax.broadcasted_iota(jnp.int32, (tt, 1), 0)
    row8 = lax.broadcasted_iota(jnp.int32, (SUBLANES, 1), 0)

    def delayed(x, prev8):
        def shifted(s):
            r = pltpu.roll(x, s, 0)
            first = jnp.where(row8 < s, pltpu.roll(prev8, s, 0), r[:SUBLANES])
            return jnp.concatenate([first, r[SUBLANES:]], axis=0)
        return shifted

    a_b = p_ref[:, 0:D_CONV].astype(F32)
    v = p_ref[:, D_CONV:2 * D_CONV].astype(F32) * p_ref[:, 2 * D_CONV:3 * D_CONV].astype(F32)
    u = _conv_taps(v, delayed(v, pa_ref[...]), caw_ref[...])
    ya_ref[...] = (a_b * u).astype(ya_ref.dtype)

    rx = p_ref[:, 3 * D_CONV:3 * D_CONV + D_LRU].astype(F32)
    xr = _conv_taps(rx, delayed(rx, pb_ref[...]), cbw_ref[...], cbb_ref[...])
    a, b = _lru_coeffs(xr, wr_ref[...], wi_ref[...], br_ref[...], bi_ref[...], lam_ref[...])
    as_ref[...], bs_ref[...] = _scan(a, b, row % SUBLANES, SUBLANES)

    def carry_group(g, h_in):
        r = pl.ds(pl.multiple_of(g * SUBLANES, SUBLANES), SUBLANES)
        hg = bs_ref[r, :] + as_ref[r, :] * h_in
        hs_ref[r, :] = hg
        return hg[SUBLANES - 1:SUBLANES, :]

    lax.fori_loop(0, tt // SUBLANES, carry_group, h_ref[...], unroll=8)
    h = hs_ref[...]
    yr_ref[...] = (h * jax.nn.gelu(ry_ref[...].astype(F32))).astype(yr_ref.dtype)

    tail = slice(tt - SUBLANES, tt)
    ta_ref[...] = v[tail]
    tb_ref[...] = rx[tail]
    th_ref[...] = h[tail]
    pa_ref[...] = v[tail]
    pb_ref[...] = rx[tail]
    h_ref[...] = h[tt - 1:tt]


def _convlru_prompt(P, w, l, B, S, tt):
    nt = S // tt
    cb = P_CHUNK // D_CONV
    lay = lambda *shape: pl.BlockSpec((None,) + shape, lambda b, t: (l,) + (0,) * len(shape))
    seq = pl.BlockSpec((tt, D_CONV), lambda b, t: (b * nt + t, 0))
    tail = pl.BlockSpec((None, SUBLANES, D_CONV), lambda b, t: (b, 0, 0))
    tail_shape = jax.ShapeDtypeStruct((B, SUBLANES, D_CONV), F32)
    return pl.pallas_call(
        _convlru_prompt_kernel,
        grid=(B, nt),
        in_specs=[
            pl.BlockSpec((tt, P_CHUNK), lambda b, t: (b * nt + t, P_CONV // P_CHUNK)),
            pl.BlockSpec((tt, D_LRU), lambda b, t: (b * nt + t, P_TAIL // D_LRU)),
            lay(3, D_CONV), lay(4, D_LRU), lay(1, D_LRU), lay(D_LRU, D_LRU), lay(D_LRU, D_LRU),
            lay(1, D_LRU), lay(1, D_LRU), lay(1, D_LRU),
        ],
        out_specs=[seq, seq, tail, tail, tail],
        out_shape=[
            jax.ShapeDtypeStruct((B * S, D_CONV), BF16),
            jax.ShapeDtypeStruct((B * S, D_LRU), BF16),
            tail_shape, tail_shape, tail_shape,
        ],
        scratch_shapes=[pltpu.VMEM((SUBLANES, D_CONV), F32), pltpu.VMEM((SUBLANES, D_LRU), F32),
                        pltpu.VMEM((1, D_LRU), F32)] + [pltpu.VMEM((tt, D_LRU), F32)] * 3,
        compiler_params=_params("parallel", "arbitrary"),
        name="convlru_prompt",
    )(P, P, w["conv_a_w"], w["conv_b_w"], w["conv_b_bias"], w["lru_wr"], w["lru_wi"], w["lru_br"],
      w["lru_bi"], w["lru_lambda"])


def _convlru_sample_kernel(p_ref, ry_ref, sa_ref, sb_ref, h0_ref, caw_ref, cbw_ref, cbb_ref, wr_ref,
                           wi_ref, br_ref, bi_ref, lam_ref, ya_ref, yr_ref, va_ref, h_ref, *, ds):
    ts = p_ref.shape[0]
    pos = lax.broadcasted_iota(jnp.int32, (ts, 1), 0) % ds

    def delayed(x, state):
        def shifted(s):
            return jnp.where(pos < s, pltpu.roll(state, ts - ds + s, 0), pltpu.roll(x, s, 0))
        return shifted

    a_b = p_ref[:, 0:D_CONV].astype(F32)
    v = p_ref[:, D_CONV:2 * D_CONV].astype(F32) * p_ref[:, 2 * D_CONV:3 * D_CONV].astype(F32)
    u = _conv_taps(v, delayed(v, sa_ref[...]), caw_ref[...])
    ya_ref[...] = (a_b * u).astype(ya_ref.dtype)
    va_ref[...] = v

    rx = p_ref[:, 3 * D_CONV:3 * D_CONV + D_LRU].astype(F32)
    xr = _conv_taps(rx, delayed(rx, sb_ref[...]), cbw_ref[...], cbb_ref[...])
    a, b = _lru_coeffs(xr, wr_ref[...], wi_ref[...], br_ref[...], bi_ref[...], lam_ref[...])
    b = b + a * h0_ref[...]
    _, h = _scan(a, b, pos, ds)
    h_ref[...] = h
    yr_ref[...] = (h * jax.nn.gelu(ry_ref[...].astype(F32))).astype(yr_ref.dtype)


def _convlru_sample(P, sa, sb, h0, w, l, row0, Ts, ds, ts):
    assert ds == SUBLANES
    r0 = row0 // ts
    lay = lambda *shape: pl.BlockSpec((None,) + shape, lambda i: (l,) + (0,) * len(shape))
    st = pl.BlockSpec((None, ts, D_CONV), lambda i: (l, i, 0))
    seq = pl.BlockSpec((ts, D_CONV), lambda i: (i, 0))
    return pl.pallas_call(
        functools.partial(_convlru_sample_kernel, ds=ds),
        grid=(Ts // ts,),
        in_specs=[
            pl.BlockSpec((ts, P_CHUNK), lambda i: (r0 + i, P_CONV // P_CHUNK)),
            pl.BlockSpec((ts, D_LRU), lambda i: (r0 + i, P_TAIL // D_LRU)),
            st, st, st,
            lay(3, D_CONV), lay(4, D_LRU), lay(1, D_LRU), lay(D_LRU, D_LRU), lay(D_LRU, D_LRU),
            lay(1, D_LRU), lay(1, D_LRU), lay(1, D_LRU),
        ],
        out_specs=[seq, seq, seq, seq],
        out_shape=[
            jax.ShapeDtypeStruct((Ts, D_CONV), BF16),
            jax.ShapeDtypeStruct((Ts, D_LRU), BF16),
            jax.ShapeDtypeStruct((Ts, D_CONV), F32),
            jax.ShapeDtypeStruct((Ts, D_LRU), F32),
        ],
        compiler_params=_params("parallel"),
        name="convlru_sample",
    )(P, P, sa, sb, h0, w["conv_a_w"], w["conv_b_w"], w["conv_b_bias"], w["lru_wr"], w["lru_wi"],
      w["lru_br"], w["lru_bi"], w["lru_lambda"])


def _flash_kernel(q_ref, k_ref, v_ref, o_ref, *, tq, ch):
    S = k_ref.shape[0]
    qi = pl.program_id(2)
    n_chunks = (qi * tq + tq + ch - 1) // ch
    first_head = lax.broadcasted_iota(jnp.int32, (1, LANES), 1) < V_DIM
    qpos = qi * tq + lax.broadcasted_iota(jnp.int32, (tq, 1), 0)

    def body(n):
        L = n * ch
        v = v_ref[0:L, :]
        one = jnp.ones_like(v)
        vext = (jnp.where(first_head, v, one), jnp.where(first_head, one, v))
        L0 = L - ch
        mask = (L0 + lax.broadcasted_iota(jnp.int32, (1, ch), 1)) <= qpos
        outs = []
        for hh in range(2):
            sl = slice(hh * HEAD_SLOT, (hh + 1) * HEAD_SLOT)
            s = _dot_nt(q_ref[:, sl], k_ref[0:L, sl])
            tail = jnp.where(mask, s[:, L0:], NEG)
            s = jnp.concatenate([s[:, :L0], tail], axis=1) if L0 else tail
            m = jnp.max(s, axis=-1, keepdims=True)
            a = _dot(jnp.exp2(s - m).astype(BF16), vext[hh])
            outs.append(a / pltpu.roll(a, V_DIM, 1))
        o_ref[...] = jnp.where(first_head, outs[0], outs[1]).astype(o_ref.dtype)

    for n in range(1, S // ch + 1):
        pl.when(n_chunks == n)(functools.partial(body, n))


def _flash(q, k, v, B, S, tq, ch):
    assert tq <= ch and S % ch == 0
    nq = S // tq
    hp = MLA_HEADS // 2
    return pl.pallas_call(
        functools.partial(_flash_kernel, tq=tq, ch=ch),
        grid=(B, hp, nq),
        in_specs=[
            pl.BlockSpec((tq, 2 * HEAD_SLOT), lambda b, h, i: (b * nq + i, h)),
            pl.BlockSpec((S, 2 * HEAD_SLOT), lambda b, h, i: (b, h)),
            pl.BlockSpec((S, 2 * V_DIM), lambda b, h, i: (b, h)),
        ],
        out_specs=pl.BlockSpec((tq, 2 * V_DIM), lambda b, h, i: (b * nq + i, h)),
        out_shape=jax.ShapeDtypeStruct((B * S, MLA_HEADS * V_DIM), BF16),
        compiler_params=_params("parallel", "parallel", "arbitrary"),
        name="flash",
    )(q, k, v)


def _sattn_kernel(pt_ref, q_ref, latn_ref, krn_ref, wkn_ref, wkt_ref, wv_ref, kg_ref, clat_ref, ckrt_ref,
                  o_ref, a_ref, lat_st, kr_st, sem, *, l, n_pages, page, ds, nsub):
    b = pl.program_id(0)
    nb = pl.num_programs(0)
    R = MLA_HEADS * ds
    n = n_pages * page
    sub = n // nsub

    def page_copies(seq, slot, i):
        pid = pt_ref[seq, i]
        dst = pl.ds(pl.multiple_of(i * page, page), page)
        return (pltpu.make_async_copy(clat_ref.at[l, pid], lat_st.at[slot, dst, :], sem.at[0, slot]),
                pltpu.make_async_copy(ckrt_ref.at[l, pid], kr_st.at[slot, :, dst], sem.at[1, slot]))

    def start_all(seq, slot):
        def body(i, c):
            for cp in page_copies(seq, slot, i):
                cp.start()
            return c
        lax.fori_loop(0, n_pages, body, 0, unroll=DMA_UNROLL)

    def wait_all(seq, slot):
        def body(i, c):
            for cp in page_copies(seq, slot, i):
                cp.wait()
            return c
        lax.fori_loop(0, n_pages, body, 0, unroll=DMA_UNROLL)

    slot = b % 2

    @pl.when(b == 0)
    def _():
        start_all(0, 0)

    @pl.when(b + 1 < nb)
    def _():
        start_all(b + 1, 1 - slot)

    qk = q_ref[...] * kg_ref[...]
    qa, qp = [], []
    for h in range(MLA_HEADS):
        qh = qk[:, h * HEAD_SLOT:(h + 1) * HEAD_SLOT]
        qa.append(_dot(qh.astype(BF16), wkt_ref[h]))
        qp.append(qh[:, 0:ROPE_DIM])
    a_ref[0:NK, :] = wkn_ref[...]
    a_ref[NK:NK + R, :] = jnp.concatenate(qa, axis=0).astype(BF16)
    qpe = jnp.concatenate(qp, axis=0).astype(BF16)
    ones_pe = jnp.ones((MLA_HEADS, ROPE_DIM), BF16)

    def update(carry, L, s_pe, ssq_pe, mask):
        m_prev, l_prev, acc = carry
        u = _dot_nt(a_ref[...], L)
        kn = u[0:NK].reshape(MLA_HEADS, NOPE_DIM, -1)
        ssq = jnp.sum(kn * kn, axis=1) + ssq_pe
        rinv = lax.rsqrt(ssq * (1.0 / QK_DIM) + EPS)
        s = ((u[NK:NK + R] + s_pe).reshape(MLA_HEADS, ds, -1) * rinv[:, None, :]).reshape(R, -1)
        if mask is not None:
            s = jnp.where(mask, s, NEG)
        m_new = jnp.maximum(m_prev, jnp.max(s, axis=-1, keepdims=True))
        al = jnp.exp2(m_prev - m_new)
        pr = jnp.exp2(s - m_new)
        l_new = al * l_prev + jnp.sum(pr, axis=-1, keepdims=True)
        return m_new, l_new, al * acc + _dot(pr.astype(BF16), L)

    Ln = jnp.concatenate([latn_ref[...], jnp.zeros((page - ds, KV_LORA), F32)], axis=0).astype(BF16)
    Rn = jnp.concatenate([krn_ref[:, 0:ROPE_DIM], jnp.zeros((page - ds, ROPE_DIM), F32)], axis=0)
    key = lax.broadcasted_iota(jnp.int32, (R, page), 1)
    qry = lax.broadcasted_iota(jnp.int32, (R, page), 0) % ds
    carry = (jnp.full((R, 1), NEG, F32), jnp.zeros((R, 1), F32), jnp.zeros((R, KV_LORA), F32))
    carry = update(carry, Ln, _dot_nt(qpe, Rn.astype(BF16)), _dot_nt(ones_pe, (Rn * Rn).astype(BF16)),
                   key <= qry)

    wait_all(b, slot)
    for c in range(nsub):
        L = lat_st[slot, c * sub:(c + 1) * sub, :].astype(BF16)
        rt = kr_st[slot, :, c * sub:(c + 1) * sub]
        rf = rt.astype(F32)
        carry = update(carry, L, _dot(qpe, rt), _dot(ones_pe, (rf * rf).astype(BF16)), None)

    _, l_fin, acc = carry
    o_lat = (acc / l_fin).astype(BF16)
    y = _dot(o_lat, wv_ref[...])
    vrow = lax.broadcasted_iota(jnp.int32, (R, MLA_HEADS * V_DIM), 0) // ds
    vcol = lax.broadcasted_iota(jnp.int32, (R, MLA_HEADS * V_DIM), 1) // V_DIM
    y = jnp.where(vrow == vcol, y, 0.0)
    o_ref[...] = jnp.sum(y.reshape(MLA_HEADS, ds, MLA_HEADS * V_DIM), axis=0)


def _sattn(page_table, q_s, lat, kr, cache_lat, cache_krt, w, l, row0, DB, ds, nsub):
    n_pages = page_table.shape[1]
    page = cache_lat.shape[2]
    assert ds == SUBLANES and MLA_HEADS == SUBLANES and (n_pages * page) % nsub == 0
    HW = MLA_HEADS * HEAD_SLOT
    R = MLA_HEADS * ds
    r0 = row0 // ds
    lay = lambda *shape: pl.BlockSpec((None,) + shape, lambda b, pt: (l,) + (0,) * len(shape))
    grid_spec = pltpu.PrefetchScalarGridSpec(
        num_scalar_prefetch=1,
        grid=(DB,),
        in_specs=[
            pl.BlockSpec((ds, HW), lambda b, pt: (b, 0)),
            pl.BlockSpec((ds, KV_LORA), lambda b, pt: (r0 + b, 0)),
            pl.BlockSpec((ds, LANES), lambda b, pt: (r0 + b, 0)),
            lay(NK, KV_LORA), lay(MLA_HEADS, HEAD_SLOT, KV_LORA), lay(KV_LORA, MLA_HEADS * V_DIM), lay(1, HW),
            pl.BlockSpec(memory_space=pl.ANY), pl.BlockSpec(memory_space=pl.ANY),
        ],
        out_specs=pl.BlockSpec((ds, MLA_HEADS * V_DIM), lambda b, pt: (b, 0)),
        scratch_shapes=[
            pltpu.VMEM((NK + R, KV_LORA), BF16),
            pltpu.VMEM((2, n_pages * page, KV_LORA), F32),
            pltpu.VMEM((2, ROPE_DIM, n_pages * page), BF16),
            pltpu.SemaphoreType.DMA((2, 2)),
        ],
    )
    return pl.pallas_call(
        functools.partial(_sattn_kernel, l=l, n_pages=n_pages, page=page, ds=ds, nsub=nsub),
        grid_spec=grid_spec,
        out_shape=jax.ShapeDtypeStruct((DB * ds, MLA_HEADS * V_DIM), F32),
        compiler_params=_params("arbitrary"),
        name="sattn",
    )(page_table, q_s, lat, kr, w["wk_nt"], w["wk_t"], w["wv"], w["k_gain_tiled"], cache_lat, cache_krt)


def _softmax_rows(s):
    m = jnp.max(s, axis=-1, keepdims=True)
    p = jnp.exp(s - m)
    return p / jnp.sum(p, axis=-1, keepdims=True)


def _mem_prompt_kernel(q_ref, k_ref, v_ref, o_ref):
    for h in range(MEM_HEADS):
        sl = slice(h * MEM_HD, (h + 1) * MEM_HD)
        p = _softmax_rows(_dot_nt(q_ref[:, sl], k_ref[:, sl].astype(BF16)))
        o_ref[:, sl] = _dot(p.astype(BF16), v_ref[:, sl].astype(BF16)).astype(o_ref.dtype)


def _mem_prompt(qm, mk, mv, l, B, S, tm):
    nt = S // tm
    kv = pl.BlockSpec((None, None, N_MEM, MEM_WIDTH), lambda b, t: (l, b, 0, 0))
    seq = pl.BlockSpec((tm, MEM_WIDTH), lambda b, t: (b * nt + t, 0))
    return pl.pallas_call(
        _mem_prompt_kernel,
        grid=(B, nt),
        in_specs=[seq, kv, kv],
        out_specs=seq,
        out_shape=jax.ShapeDtypeStruct((B * S, MEM_WIDTH), BF16),
        compiler_params=_params("parallel", "parallel"),
        name="mem_prompt",
    )(qm, mk, mv)


def _mem_sample_kernel(q_ref, k_ref, v_ref, o_ref, *, nb, ds):
    q = q_ref[...].reshape(nb, ds, MEM_WIDTH)
    outs = []
    for h in range(MEM_HEADS):
        sl = slice(h * MEM_HD, (h + 1) * MEM_HD)
        s = jnp.einsum('bqd,bkd->bqk', q[:, :, sl], k_ref[:, :, sl].astype(BF16),
                       preferred_element_type=F32)
        p = _softmax_rows(s)
        outs.append(jnp.einsum('bqk,bkd->bqd', p.astype(BF16), v_ref[:, :, sl].astype(BF16),
                               preferred_element_type=F32))
    o_ref[...] = jnp.concatenate(outs, axis=-1).reshape(nb * ds, MEM_WIDTH)


def _mem_sample(qm_s, ck, cv, l, DB, ds, nb):
    kv = pl.BlockSpec((None, nb, N_MEM, MEM_WIDTH), lambda i: (l, i, 0, 0))
    seq = pl.BlockSpec((nb * ds, MEM_WIDTH), lambda i: (i, 0))
    return pl.pallas_call(
        functools.partial(_mem_sample_kernel, nb=nb, ds=ds),
        grid=(DB // nb,),
        in_specs=[seq, kv, kv],
        out_specs=seq,
        out_shape=jax.ShapeDtypeStruct((DB * ds, MEM_WIDTH), F32),
        compiler_params=_params("parallel"),
        name="mem_sample",
    )(qm_s, ck, cv)


def _memkv_kernel(x_ref, n_ref, w_ref, kg_ref, k_ref, v_ref):
    kv = _dot(_rms(x_ref[...], n_ref[...]).astype(BF16), w_ref[...])
    _head_norm_store(k_ref, kv[:, :MEM_WIDTH], kg_ref[...], MEM_HEADS, 1.0 / MEM_HD)
    v_ref[...] = kv[:, MEM_WIDTH:]


def _memkv(mem, mem_norm, w_kv, k_gain):
    depth = w_kv.shape[0]
    R, D = mem.shape
    lay = lambda *shape: pl.BlockSpec((None,) + shape, lambda l: (l,) + (0,) * len(shape))
    out = jax.ShapeDtypeStruct((depth, R, MEM_WIDTH), F32)
    return pl.pallas_call(
        _memkv_kernel,
        grid=(depth,),
        in_specs=[pl.BlockSpec((R, D), lambda l: (0, 0)), lay(1, D), lay(D, 2 * MEM_WIDTH), lay(1, MEM_HD)],
        out_specs=[lay(R, MEM_WIDTH), lay(R, MEM_WIDTH)],
        out_shape=[out, out],
        compiler_params=_params("parallel"),
        name="memkv",
    )(mem, mem_norm, w_kv, k_gain)


def _merge_kernel(x_ref, sg_ref, yap_ref, yrp_ref, ycp_ref, ymp_ref, yas_ref, yrs_ref, ycs_ref, yms_ref,
                  wb_ref, wo_ref, o_ref, *, n_prompt_tiles):
    D = x_ref.shape[1]

    def run(y_refs):
        merged = None
        for bi, y_ref in enumerate(y_refs):
            width = y_ref.shape[1]
            t = sg_ref[:, bi * D:(bi + 1) * D].astype(F32) * _dot(
                y_ref[...].astype(BF16), wb_ref[bi * width:(bi + 1) * width, :])
            merged = t if merged is None else merged + t
        o_ref[...] = x_ref[...] + _dot(merged.astype(BF16), wo_ref[...])

    is_prompt = pl.program_id(0) < n_prompt_tiles
    pl.when(is_prompt)(lambda: run((yap_ref, yrp_ref, ycp_ref, ymp_ref)))
    pl.when(jnp.logical_not(is_prompt))(lambda: run((yas_ref, yrs_ref, ycs_ref, yms_ref)))


def _merge(x, P, y_prompt, y_sample, wb, wo, l, tm):
    T, D = x.shape
    npt = y_prompt[0].shape[0] // tm
    row = lambda width: pl.BlockSpec((tm, width), lambda i: (i, 0))
    prow = lambda y: pl.BlockSpec((tm, y.shape[1]), lambda i: (jnp.minimum(i, npt - 1), 0))
    srow = lambda y: pl.BlockSpec((tm, y.shape[1]), lambda i: (jnp.maximum(i - npt, 0), 0))
    lay = lambda *shape: pl.BlockSpec((None,) + shape, lambda i: (l,) + (0,) * len(shape))
    return pl.pallas_call(
        functools.partial(_merge_kernel, n_prompt_tiles=npt),
        grid=(T // tm,),
        in_specs=[row(D), row(N_BRANCH * D)] + [prow(y) for y in y_prompt] + [srow(y) for y in y_sample]
                 + [lay(wb.shape[1], D), lay(D, D)],
        out_specs=row(D),
        out_shape=jax.ShapeDtypeStruct((T, D), F32),
        compiler_params=_params("parallel"),
        name="merge",
    )(x, P, *y_prompt, *y_sample, wb, wo)


def _prepare_weights(a):
    depth, D = a["norm_ffn1"].shape
    bf = lambda t: t.astype(BF16)
    vec = lambda t: t.reshape(depth, 1, -1)
    w = {}
    for name in ("ffn1", "ffn2"):
        w_in = a[name + "_w_in"]
        w[name + "_g"] = bf(w_in[:, :, :D_FF])
        w[name + "_u"] = bf(w_in[:, :, D_FF:])
        w[name + "_d"] = bf(a[name + "_w_out"])
    w["norm_ffn1"], w["norm_ffn2"], w["norm_mix"] = vec(a["norm_ffn1"]), vec(a["norm_ffn2"]), vec(a["norm_mix"])

    splits = (D_CONV, D_CONV, D_CONV, D_LRU, D_LRU, Q_LORA, KV_LORA, ROPE_DIM, MEM_WIDTH) + (D,) * N_BRANCH
    off = np.concatenate([[0], np.cumsum(splits)])
    w_in = a["w_in"]
    piece = lambda i: w_in[:, :, off[i]:off[i + 1]]
    zeros = lambda n: jnp.zeros((depth, D, n), w_in.dtype)
    half = ROPE_DIM // 2
    c_kr = piece(7)
    kr_a = jnp.concatenate([c_kr, zeros(LANES - ROPE_DIM)], axis=-1)
    kr_b = jnp.concatenate([-c_kr[..., half:], c_kr[..., :half], zeros(LANES - ROPE_DIM)], axis=-1)
    w["w_in"] = bf(jnp.concatenate(
        [w_in[:, :, off[9]:]] + [piece(i) for i in range(7)] + [kr_a, kr_b, piece(8), zeros(LANES)], axis=-1))
    assert w["w_in"].shape[-1] == P_WIDTH

    def slot_gain(g):
        return jnp.concatenate([g[:, NOPE_DIM:], g[:, :NOPE_DIM],
                                jnp.zeros((depth, HEAD_SLOT - QK_DIM), g.dtype)], axis=-1)[:, None, :]

    uq = a["mla_w_uq"].reshape(depth, Q_LORA, MLA_HEADS, QK_DIM)
    nope, pe = uq[..., :NOPE_DIM], uq[..., NOPE_DIM:]
    z = lambda n: jnp.zeros((depth, Q_LORA, MLA_HEADS, n), uq.dtype)
    w["uq_a"] = bf(jnp.concatenate([pe, nope, z(HEAD_SLOT - QK_DIM)], axis=-1).reshape(depth, Q_LORA, -1))
    w["uq_b"] = bf(jnp.concatenate([-pe[..., half:], pe[..., :half], z(HEAD_SLOT - ROPE_DIM)],
                                   axis=-1).reshape(depth, Q_LORA, -1))
    w["q_norm"], w["kv_norm"] = vec(a["mla_q_norm"]), vec(a["mla_kv_norm"])
    w["q_gain"] = slot_gain(a["mla_q_gain"])
    w["k_gain"] = slot_gain(a["mla_k_gain"])
    w["k_gain_tiled"] = jnp.tile(w["k_gain"], (1, 1, MLA_HEADS))
    w["mq_gain"] = vec(a["mem_q_gain"])

    ukv = a["mla_w_ukv"].reshape(depth, KV_LORA, MLA_HEADS, NOPE_DIM + V_DIM)
    k_nope, vv = ukv[..., :NOPE_DIM], ukv[..., NOPE_DIM:]
    zk = lambda n: jnp.zeros((depth, KV_LORA, MLA_HEADS, n), ukv.dtype)
    wk_slot = jnp.concatenate([zk(ROPE_DIM), k_nope, zk(HEAD_SLOT - QK_DIM)], axis=-1)
    w["wk_slot"] = bf(wk_slot.reshape(depth, KV_LORA, -1))
    w["wk_t"] = bf(wk_slot.transpose(0, 2, 3, 1))
    w["wk_nt"] = bf(k_nope.reshape(depth, KV_LORA, -1).transpose(0, 2, 1))
    w["wv"] = bf(vv.reshape(depth, KV_LORA, -1))

    gw = a["lru_gate_w"]
    blk = D_LRU // LRU_BLOCKS
    eye = jnp.eye(LRU_BLOCKS, dtype=gw.dtype)[None, :, None, :, None]
    dense = lambda part: (eye * part[:, :, :, None, :]).reshape(depth, D_LRU, D_LRU)
    w["lru_wr"], w["lru_wi"] = bf(dense(gw[..., :blk])), bf(dense(gw[..., blk:]))
    gb = a["lru_gate_b"]
    w["lru_br"] = gb[..., :blk].reshape(depth, 1, D_LRU)
    w["lru_bi"] = gb[..., blk:].reshape(depth, 1, D_LRU)
    w["lru_lambda"], w["conv_b_bias"] = vec(a["lru_lambda"]), vec(a["conv_b_bias"])
    w["conv_a_w"], w["conv_b_w"] = a["conv_a_w"], a["conv_b_w"]

    w["mem_norm"], w["mem_k_gain"] = vec(a["mem_norm"]), vec(a["mem_k_gain"])
    w["mem_w_kv"] = bf(a["mem_w_kv"])
    w["w_branch"], w["w_out"] = bf(a["w_branch"]), bf(a["w_out"])
    return w


def _rope_tables(pos):
    half = ROPE_DIM // 2
    inv = ROPE_THETA ** (-jnp.arange(half, dtype=F32) / half)
    ang = pos.astype(F32)[:, None] * inv[None, :]
    cos, sin = jnp.cos(ang), jnp.sin(ang)
    n = pos.shape[0]
    cos_t = jnp.concatenate([cos, cos, jnp.ones((n, NOPE_DIM), F32), jnp.zeros((n, HEAD_SLOT - QK_DIM), F32)], 1)
    sin_t = jnp.concatenate([sin, sin, jnp.zeros((n, HEAD_SLOT - ROPE_DIM), F32)], axis=1)
    return cos_t, sin_t


def _tile_sizes(T, S, Ts):
    pick = lambda n, opts: next(o for o in opts if n % o == 0)
    return dict(
        tm=pick(np.gcd(T, S), (512, 256, 128)),
        tt=pick(S, (512, 256, 128)),
        ts=pick(Ts, (256, 128, 64, 32, 16, 8)),
        tq=pick(S, (256, 128)),
    )


def kernel(x_prompt, x_sample, cache_kv_latent, cache_k_rope, cache_mem_k, cache_mem_v, state_conv_a, state_conv_b, state_lru_h, page_table, mem_prompt, norm_ffn1, ffn1_w_in, ffn1_w_out, norm_mix, w_in, conv_a_w, conv_b_w, conv_b_bias, lru_gate_w, lru_gate_b, lru_lambda, mla_q_norm, mla_w_uq, mla_kv_norm, mla_w_ukv, mla_q_gain, mla_k_gain, mem_norm, mem_w_kv, mem_k_gain, mem_q_gain, w_branch, w_out, norm_ffn2, ffn2_w_in, ffn2_w_out):
    B, S, D = x_prompt.shape
    DB, DS, _ = x_sample.shape
    depth = norm_ffn1.shape[0]
    page = cache_kv_latent.shape[2]
    past = page_table.shape[1] * page
    Tp, Ts = B * S, DB * DS
    T = Tp + Ts
    ts_ = _tile_sizes(T, S, Ts)
    tm, tt, ts, tq = ts_["tm"], ts_["tt"], ts_["ts"], ts_["tq"]
    assert Tp % ts == 0 and Tp % DS == 0

    w = _prepare_weights(dict(
        norm_ffn1=norm_ffn1, ffn1_w_in=ffn1_w_in, ffn1_w_out=ffn1_w_out, norm_mix=norm_mix, w_in=w_in,
        conv_a_w=conv_a_w, conv_b_w=conv_b_w, conv_b_bias=conv_b_bias, lru_gate_w=lru_gate_w,
        lru_gate_b=lru_gate_b, lru_lambda=lru_lambda, mla_q_norm=mla_q_norm, mla_w_uq=mla_w_uq,
        mla_kv_norm=mla_kv_norm, mla_w_ukv=mla_w_ukv, mla_q_gain=mla_q_gain, mla_k_gain=mla_k_gain,
        mem_norm=mem_norm, mem_w_kv=mem_w_kv, mem_k_gain=mem_k_gain, mem_q_gain=mem_q_gain,
        w_branch=w_branch, w_out=w_out, norm_ffn2=norm_ffn2, ffn2_w_in=ffn2_w_in, ffn2_w_out=ffn2_w_out))

    pos = jnp.concatenate([jnp.tile(jnp.arange(S), B), jnp.tile(past + jnp.arange(DS), DB)])
    cos_t, sin_t = _rope_tables(pos)

    ka, kb = state_conv_a.shape[2], state_conv_b.shape[2]
    sa = jnp.pad(state_conv_a, ((0, 0), (0, 0), (DS - ka, 0), (0, 0))).reshape(depth, Ts, D_CONV)
    sb = jnp.pad(state_conv_b, ((0, 0), (0, 0), (DS - kb, 0), (0, 0))).reshape(depth, Ts, D_LRU)
    h0 = jnp.pad(state_lru_h[:, :, None, :], ((0, 0), (0, 0), (0, DS - 1), (0, 0))).reshape(depth, Ts, D_LRU)
    cmk = cache_mem_k.reshape(depth, DB, N_MEM, MEM_WIDTH)
    cmv = cache_mem_v.reshape(depth, DB, N_MEM, MEM_WIDTH)
    cache_krt = jnp.swapaxes(cache_k_rope, 2, 3).astype(BF16)

    mk_p, mv_p = _memkv(mem_prompt.reshape(B * N_MEM, D), w["mem_norm"], w["mem_w_kv"], w["mem_k_gain"])
    mk_p4 = mk_p.reshape(depth, B, N_MEM, MEM_WIDTH)
    mv_p4 = mv_p.reshape(depth, B, N_MEM, MEM_WIDTH)

    x = jnp.concatenate([x_prompt.reshape(Tp, D), x_sample.reshape(Ts, D)], axis=0)
    p_lat, p_kr, p_ca, p_cb, p_h, s_lat, s_kr, s_ca, s_cb, s_h = ([] for _ in range(10))
    n_pages = page_table.shape[1]
    nsub = next(n for n in (4, 2, 1) if n_pages % n == 0)
    nb = next(n for n in (8, 4, 2, 1) if DB % n == 0)
    ch = max(S // 8, tq)
    for l in range(depth):
        x = _ffn(x, w["norm_ffn1"], w["ffn1_g"], w["ffn1_u"], w["ffn1_d"], l, tm)
        P = _proj(x, w["norm_mix"], w["w_in"], l, tm)
        q, lat, kr, k, v, qm = _prep(P, cos_t, sin_t, w, l, tm)
        ya_p, yr_p, ta, tb, th = _convlru_prompt(P, w, l, B, S, tt)
        ya_s, yr_s, va_s, h_s = _convlru_sample(P, sa, sb, h0, w, l, Tp, Ts, DS, ts)
        yc_p = _flash(q, k, v, B, S, tq, ch)
        yc_s = _sattn(page_table, q[Tp:].astype(F32), lat, kr, cache_kv_latent, cache_krt, w, l,
                      Tp, DB, DS, nsub)
        ym_p = _mem_prompt(qm, mk_p4, mv_p4, l, B, S, tm)
        ym_s = _mem_sample(qm[Tp:], cmk, cmv, l, DB, DS, nb)
        x = _merge(x, P, (ya_p, yr_p, yc_p, ym_p), (ya_s, yr_s, yc_s, ym_s), w["w_branch"], w["w_out"], l, tm)
        x = _ffn(x, w["norm_ffn2"], w["ffn2_g"], w["ffn2_u"], w["ffn2_d"], l, tm)

        p_lat.append(lat[:Tp])
        s_lat.append(lat[Tp:])
        p_kr.append(kr[:Tp, :ROPE_DIM])
        s_kr.append(kr[Tp:, :ROPE_DIM])
        p_ca.append(ta[:, SUBLANES - ka:])
        p_cb.append(tb[:, SUBLANES - kb:])
        p_h.append(th[:, SUBLANES - 1])
        s_ca.append(va_s.reshape(DB, DS, D_CONV)[:, DS - ka:])
        s_cb.append(P[Tp:, P_CONV + 3 * D_CONV:P_CONV + 3 * D_CONV + D_LRU].reshape(DB, DS, D_LRU)[:, DS - kb:].astype(F32))
        s_h.append(h_s.reshape(DB, DS, D_LRU)[:, DS - 1])

    return (
        x[:Tp].reshape(B, S, D), x[Tp:].reshape(DB, DS, D),
        jnp.stack(p_lat).reshape(depth, B, S, KV_LORA), jnp.stack(p_kr).reshape(depth, B, S, ROPE_DIM),
        jnp.stack(p_ca), jnp.stack(p_cb), jnp.stack(p_h),
        mk_p.reshape(depth, B, N_MEM, MEM_HEADS, MEM_HD), mv_p.reshape(depth, B, N_MEM, MEM_HEADS, MEM_HD),
        jnp.stack(s_lat).reshape(depth, DB, DS, KV_LORA), jnp.stack(s_kr).reshape(depth, DB, DS, ROPE_DIM),
        jnp.stack(s_ca), jnp.stack(s_cb), jnp.stack(s_h),
    )
```
